```python
import jax
import jax.numpy as jnp
from jax import lax
import numpy as np

D_MODEL = 1024
BATCH = 32
SEQ = 2048
DEPTH = 4

D_MIX = 1024
M_HEADS = 4
M_HEAD_DIM = 96
M_WIDTH = M_HEADS * M_HEAD_DIM
M_CHUNK = 128
CONV_WIDTH = 4
A_HEADS = 6
A_NOPE = 64
A_ROPE = 32
A_QK_DIM = A_NOPE + A_ROPE
A_V_DIM = 64
A_WIDTH = A_HEADS * A_V_DIM
A_Q_RANK = 256
A_KV_RANK = 128
ROPE_THETA = 10000.0
Q_BLOCK = 128
C_GROUPS = 4
C_GROUP_DIM = 64
C_WIDTH = C_GROUPS * C_GROUP_DIM
C_CHUNK = 128
IN_SIZES = (M_WIDTH, M_WIDTH, M_WIDTH, M_WIDTH, M_HEADS, M_HEADS,
            A_Q_RANK, A_KV_RANK, A_ROPE, C_WIDTH, C_WIDTH)
D_IN = 4 * M_WIDTH + 2 * M_HEADS + A_Q_RANK + A_KV_RANK + A_ROPE + 2 * C_WIDTH
N_GROUPS = 4
EXPERTS_PER_GROUP = 8
N_EXPERTS = N_GROUPS * EXPERTS_PER_GROUP
TOP_K = 2
D_EXPERT = 256
MOE_BLOCK = 256
EPS = 1e-6

kernel_name = 'hybrid_mlstm_mla_gmlp_hmoe'


def rms_norm(x, gain):
    xf = x.astype(jnp.float32)
    y = xf * lax.rsqrt(jnp.mean(xf * xf, axis=-1, keepdims=True) + EPS)
    return (y * gain.astype(jnp.float32)).astype(x.dtype)


def head_rms_norm(x, gain, n_heads):
    b, s, w = x.shape
    y = rms_norm(x.reshape(b, s, n_heads, w // n_heads), gain.reshape(n_heads, w // n_heads))
    return y.reshape(b, s, w)


def split_cols(t, sizes):
    out, off = [], 0
    for size in sizes:
        out.append(t[..., off:off + size])
        off += size
    return out


def causal_depthwise_conv(x, w, b):
    width, seq = w.shape[0], x.shape[1]
    xp = jnp.pad(x, ((0, 0), (width - 1, 0), (0, 0)))
    y = b
    for j in range(width):
        y = y + xp[:, j:j + seq] * w[j]
    return y


def mlstm_chunkwise(q, k, v, i_pre, f_pre):
    B, S, H, Dh = q.shape
    L = M_CHUNK
    N = S // L
    f32 = jnp.float32
    def to_chunks(t):
        return t.astype(f32).reshape(B, N, L, H, -1).transpose(1, 0, 3, 2, 4)
    qc = to_chunks(q)
    kc = to_chunks(k) * (Dh ** -0.5)
    vc = to_chunks(v)
    log_f = jax.nn.log_sigmoid(f_pre.astype(f32)).reshape(B, N, L, H).transpose(1, 0, 3, 2)
    log_i = i_pre.astype(f32).reshape(B, N, L, H).transpose(1, 0, 3, 2)
    bcum = jnp.cumsum(log_f, axis=-1)
    causal = jnp.tril(jnp.ones((L, L), dtype=bool))

    def step(carry, inp):
        C, n, m = carry
        qt, kt, vt, bt, it = inp
        d = bt[..., :, None] - bt[..., None, :] + it[..., None, :]
        d = jnp.where(causal, d, -jnp.inf)
        inter = bt + m[..., None]
        m_row = jnp.maximum(inter, jnp.max(d, axis=-1))
        w_intra = jnp.exp(d - m_row[..., None])
        w_inter = jnp.exp(inter - m_row)
        s = jnp.einsum('bhld,bhsd->bhls', qt, kt) * w_intra
        num = (jnp.einsum('bhls,bhsd->bhld', s, vt)
               + w_inter[..., None] * jnp.einsum('bhvk,bhlk->bhlv', C, qt))
        den = jnp.sum(s, axis=-1) + w_inter * jnp.einsum('bhk,bhlk->bhl', n, qt)
        h = num / jnp.maximum(jnp.abs(den), jnp.exp(-m_row))[..., None]
        b_last = bt[..., -1]
        g = b_last[..., None] - bt + it
        m_new = jnp.maximum(b_last + m, jnp.max(g, axis=-1))
        a = jnp.exp(b_last + m - m_new)
        wg = jnp.exp(g - m_new[..., None])
        C_new = a[..., None, None] * C + jnp.einsum('bhs,bhsv,bhsk->bhvk', wg, vt, kt)
        n_new = a[..., None] * n + jnp.einsum('bhs,bhsk->bhk', wg, kt)
        return (C_new, n_new, m_new), h

    init = (jnp.zeros((B, H, Dh, Dh), f32), jnp.zeros((B, H, Dh), f32), jnp.zeros((B, H), f32))
    _, h = lax.scan(step, init, (qc, kc, vc, bcum, log_i))
    return h.transpose(1, 0, 3, 2, 4).reshape(B, S, H, Dh).astype(v.dtype)


def rope_tables(positions):
    inv_freq = 1.0 / (ROPE_THETA ** (jnp.arange(0, A_ROPE, 2, dtype=jnp.float32) / A_ROPE))
    ang = positions.astype(jnp.float32)[..., None] * inv_freq
    return jnp.cos(ang), jnp.sin(ang)


def apply_rope(x, cos, sin):
    half = x.shape[-1] // 2
    xf = x.astype(jnp.float32)
    c, s = cos[:, :, None, :], sin[:, :, None, :]
    x1, x2 = xf[..., :half], xf[..., half:]
    return jnp.concatenate([x1 * c - x2 * s, x1 * s + x2 * c], axis=-1).astype(x.dtype)


def causal_block_attention(q, k, v):
    B, S, H, Dq = q.shape
    nb = S // Q_BLOCK
    scale = Dq ** -0.5
    qb = q.reshape(B, nb, Q_BLOCK, H, Dq).transpose(1, 0, 2, 3, 4)
    kpos = jnp.arange(S)

    def one_block(args):
        qi, idx = args
        s = jnp.einsum('blhd,bshd->bhls', qi, k, preferred_element_type=jnp.float32) * scale
        qpos = idx * Q_BLOCK + jnp.arange(Q_BLOCK)
        s = jnp.where(kpos[None, :] <= qpos[:, None], s, -jnp.inf)
        p = jax.nn.softmax(s, axis=-1).astype(v.dtype)
        return jnp.einsum('bhls,bshd->blhd', p, v)

    o = lax.map(one_block, (qb, jnp.arange(nb)))
    return o.transpose(1, 0, 2, 3, 4).reshape(B, S, H, v.shape[-1])


def mla_mixer(c_q, c_kv, k_rope, cos, sin, q_gain, kv_gain, w_uq, w_ukv, qh_gain, kh_gain):
    B, S, _ = c_q.shape
    q = (rms_norm(c_q, q_gain) @ w_uq).reshape(B, S, A_HEADS, A_QK_DIM)
    kv = (rms_norm(c_kv, kv_gain) @ w_ukv).reshape(B, S, A_HEADS, A_NOPE + A_V_DIM)
    k_nope, v = kv[..., :A_NOPE], kv[..., A_NOPE:]
    k = jnp.concatenate([k_nope, jnp.broadcast_to(k_rope[:, :, None, :], (B, S, A_HEADS, A_ROPE))], axis=-1)
    q = rms_norm(q, qh_gain)
    k = rms_norm(k, kh_gain)
    q = jnp.concatenate([q[..., :A_NOPE], apply_rope(q[..., A_NOPE:], cos, sin)], axis=-1)
    k = jnp.concatenate([k[..., :A_NOPE], apply_rope(k[..., A_NOPE:], cos, sin)], axis=-1)
    o = causal_block_attention(q, k, v)
    return o.reshape(B, S, A_WIDTH)


def chunk_spatial_gating(u, v, v_gain, w_s, b_s):
    B, S, _ = u.shape
    n = S // C_CHUNK
    v = rms_norm(v.reshape(B, S, C_GROUPS, C_GROUP_DIM), v_gain.reshape(C_GROUPS, C_GROUP_DIM))
    vc = v.reshape(B, n, C_CHUNK, C_GROUPS, C_GROUP_DIM)
    w = w_s * jnp.tril(jnp.ones((C_CHUNK, C_CHUNK), dtype=w_s.dtype))
    mixed = jnp.einsum('gts,bnsgd->bntgd', w, vc) + b_s.T[None, None, :, :, None]
    return u * mixed.reshape(B, S, C_WIDTH)


def hier_moe(x, w_group, b_group, w_expert, b_expert, w1, w3, w2):
    B, S, D = x.shape
    T = B * S
    xt = x.reshape(T, D)
    f32 = jnp.float32
    g_logits = (xt @ w_group).astype(f32)
    g_prob = jax.nn.softmax(g_logits, axis=-1)
    _, g_sel = lax.top_k(g_logits + b_group.astype(f32), 1)
    g_sel = g_sel[:, 0]
    tok = jnp.arange(T)
    g_gate = g_prob[tok, g_sel]
    e_logits = (xt @ w_expert).astype(f32).reshape(T, N_GROUPS, EXPERTS_PER_GROUP)
    e_logits = e_logits[tok, g_sel]
    e_bias = b_expert.astype(f32).reshape(N_GROUPS, EXPERTS_PER_GROUP)[g_sel]
    e_prob = jax.nn.softmax(e_logits, axis=-1)
    _, e_local = lax.top_k(e_logits + e_bias, TOP_K)
    e_w = jnp.take_along_axis(e_prob, e_local, axis=1)
    e_w = e_w / jnp.sum(e_w, axis=-1, keepdims=True)
    weights = g_gate[:, None] * e_w
    expert_id = g_sel[:, None] * EXPERTS_PER_GROUP + e_local

    A = T * TOP_K
    flat_e = expert_id.reshape(A)
    flat_tok = jnp.arange(A, dtype=jnp.int32) // TOP_K
    flat_w = weights.reshape(A)
    order = jnp.argsort(flat_e)
    se, stok, sw = flat_e[order], flat_tok[order], flat_w[order]
    counts = jnp.bincount(flat_e, length=N_EXPERTS)
    starts = jnp.cumsum(counts) - counts
    padded = ((counts + MOE_BLOCK - 1) // MOE_BLOCK) * MOE_BLOCK
    pad_ends = jnp.cumsum(padded)
    pad_starts = pad_ends - padded
    dest = pad_starts[se] + jnp.arange(A) - starts[se]
    n_blocks = -(-A // MOE_BLOCK) + N_EXPERTS
    P = n_blocks * MOE_BLOCK
    row_tok = jnp.full((P,), T, dtype=jnp.int32).at[dest].set(stok)
    row_w = jnp.zeros((P,), dtype=x.dtype).at[dest].set(sw.astype(x.dtype))
    block_start = jnp.arange(n_blocks) * MOE_BLOCK
    block_e = jnp.minimum(jnp.sum(pad_ends[None, :] <= block_start[:, None], axis=1), N_EXPERTS - 1)
    x_pad = jnp.concatenate([xt, jnp.zeros((1, D), dtype=x.dtype)], axis=0)

    def expert_block(args):
        rows, rw, e = args
        xi = x_pad[rows]
        h = jax.nn.silu(xi @ w1[e]) * (xi @ w3[e])
        return (h @ w2[e]) * rw[:, None]

    yb = lax.map(expert_block, (row_tok.reshape(n_blocks, MOE_BLOCK), row_w.reshape(n_blocks, MOE_BLOCK), block_e))
    out = jnp.zeros((T + 1, D), dtype=x.dtype).at[row_tok].add(yb.reshape(P, D))[:T]
    return out.reshape(B, S, D)


def setup_inputs(seed: int = 0) -> dict:
    key = jax.random.key(seed)
    ks = jax.random.split(key, 32)
    nrm = jax.random.normal
    def gain(k, n):
        return 1.0 + 0.02 * nrm(k, (DEPTH, n), jnp.float32)
    x = nrm(ks[0], (BATCH, SEQ, D_MODEL), jnp.float32)
    offsets = jax.random.randint(ks[1], (BATCH, 1), 0, 4096, dtype=jnp.int32)
    positions = offsets + jnp.arange(SEQ, dtype=jnp.int32)[None, :]
    f_bias = jnp.linspace(3.0, 6.0, M_HEADS, dtype=jnp.float32)[None, :]
    m_gate_bias = jnp.concatenate([0.1 * nrm(ks[6], (DEPTH, M_HEADS), jnp.float32),
                                   f_bias + 0.1 * nrm(ks[7], (DEPTH, M_HEADS), jnp.float32)], axis=-1)
    return {
        'x': x,
        'positions': positions,
        'attn_norm': gain(ks[2], D_MODEL),
        'w_in': nrm(ks[3], (DEPTH, D_MODEL, D_IN), jnp.float32) * D_MODEL ** -0.5,
        'm_conv_w': nrm(ks[4], (DEPTH, CONV_WIDTH, 2 * M_WIDTH), jnp.float32) * CONV_WIDTH ** -0.5,
        'm_conv_b': 0.02 * nrm(ks[5], (DEPTH, 2 * M_WIDTH), jnp.float32),
        'm_gate_bias': m_gate_bias,
        'a_q_norm': gain(ks[8], A_Q_RANK),
        'a_kv_norm': gain(ks[9], A_KV_RANK),
        'a_w_uq': nrm(ks[10], (DEPTH, A_Q_RANK, A_HEADS * A_QK_DIM), jnp.float32) * A_Q_RANK ** -0.5,
        'a_w_ukv': nrm(ks[11], (DEPTH, A_KV_RANK, A_HEADS * (A_NOPE + A_V_DIM)), jnp.float32) * A_KV_RANK ** -0.5,
        'a_q_head_norm': gain(ks[12], A_QK_DIM),
        'a_k_head_norm': gain(ks[13], A_QK_DIM),
        'c_v_norm': gain(ks[14], C_WIDTH),
        'c_w_s': nrm(ks[15], (DEPTH, C_GROUPS, C_CHUNK, C_CHUNK), jnp.float32) * C_CHUNK ** -0.5,
        'c_b_s': 1.0 + 0.1 * nrm(ks[16], (DEPTH, C_GROUPS, C_CHUNK), jnp.float32),
        'mix_out_norm': gain(ks[17], D_MIX),
        'w_out': nrm(ks[18], (DEPTH, D_MIX, D_MODEL), jnp.float32) * (D_MIX ** -0.5) * ((2 * DEPTH) ** -0.5),
        'ffn_norm': gain(ks[19], D_MODEL),
        'w_group': nrm(ks[20], (DEPTH, D_MODEL, N_GROUPS), jnp.float32) * D_MODEL ** -0.5,
        'b_group': 0.01 * nrm(ks[21], (DEPTH, N_GROUPS), jnp.float32),
        'w_expert': nrm(ks[22], (DEPTH, D_MODEL, N_EXPERTS), jnp.float32) * D_MODEL ** -0.5,
        'b_expert': 0.01 * nrm(ks[23], (DEPTH, N_EXPERTS), jnp.float32),
        'w1': nrm(ks[24], (DEPTH, N_EXPERTS, D_MODEL, D_EXPERT), jnp.float32) * D_MODEL ** -0.5,
        'w3': nrm(ks[25], (DEPTH, N_EXPERTS, D_MODEL, D_EXPERT), jnp.float32) * D_MODEL ** -0.5,
        'w2': nrm(ks[26], (DEPTH, N_EXPERTS, D_EXPERT, D_MODEL), jnp.float32) * D_EXPERT ** -0.5 * ((2 * DEPTH) ** -0.5),
    }


def reference(x, positions, attn_norm, w_in, m_conv_w, m_conv_b, m_gate_bias, a_q_norm, a_kv_norm,
              a_w_uq, a_w_ukv, a_q_head_norm, a_k_head_norm, c_v_norm, c_w_s, c_b_s, mix_out_norm,
              w_out, ffn_norm, w_group, b_group, w_expert, b_expert, w1, w3, w2):
    B, S, _ = x.shape
    cos, sin = rope_tables(positions)
    for l in range(DEPTH):
        h = rms_norm(x, attn_norm[l])
        proj = h @ w_in[l]
        mq, mk, mv, mo, mi, mf, cq, ckv, krope, cu, cv = split_cols(proj, IN_SIZES)
        qk = jax.nn.silu(causal_depthwise_conv(jnp.concatenate([mq, mk], axis=-1), m_conv_w[l], m_conv_b[l]))
        mq, mk = qk[..., :M_WIDTH], qk[..., M_WIDTH:]
        gb = m_gate_bias[l]
        hm = mlstm_chunkwise(mq.reshape(B, S, M_HEADS, M_HEAD_DIM),
                             mk.reshape(B, S, M_HEADS, M_HEAD_DIM),
                             mv.reshape(B, S, M_HEADS, M_HEAD_DIM),
                             mi + gb[:M_HEADS], mf + gb[M_HEADS:])
        hm = jax.nn.sigmoid(mo) * hm.reshape(B, S, M_WIDTH)
        ha = mla_mixer(cq, ckv, krope, cos, sin, a_q_norm[l], a_kv_norm[l], a_w_uq[l], a_w_ukv[l],
                       a_q_head_norm[l], a_k_head_norm[l])
        hc = chunk_spatial_gating(jax.nn.gelu(cu), jax.nn.gelu(cv), c_v_norm[l], c_w_s[l], c_b_s[l])
        g = mix_out_norm[l]
        y = jnp.concatenate([head_rms_norm(hm, g[:M_WIDTH], M_HEADS),
                             head_rms_norm(ha, g[M_WIDTH:M_WIDTH + A_WIDTH], A_HEADS),
                             head_rms_norm(hc, g[M_WIDTH + A_WIDTH:], C_GROUPS)], axis=-1)
        x = x + y @ w_out[l]
        x = x + hier_moe(rms_norm(x, ffn_norm[l]), w_group[l], b_group[l], w_expert[l], b_expert[l],
                         w1[l], w3[l], w2[l])
    return x
```

```python
import functools

import jax
import jax.numpy as jnp
from jax import lax
from jax.experimental import pallas as pl
from jax.experimental.pallas import tpu as pltpu

D_MODEL = 1024
M_HEADS = 4
M_HEAD_DIM = 96
M_WIDTH = M_HEADS * M_HEAD_DIM
M_CHUNK = 128
CONV_WIDTH = 4
A_HEADS = 6
A_NOPE = 64
A_ROPE = 32
A_QK_DIM = A_NOPE + A_ROPE
A_V_DIM = 64
A_WIDTH = A_HEADS * A_V_DIM
A_Q_RANK = 256
A_KV_RANK = 128
ROPE_THETA = 10000.0
C_GROUPS = 4
C_GROUP_DIM = 64
C_WIDTH = C_GROUPS * C_GROUP_DIM
C_CHUNK = 128
N_GROUPS = 4
EXPERTS_PER_GROUP = 8
N_EXPERTS = N_GROUPS * EXPERTS_PER_GROUP
D_EXPERT = 256
EPS = 1e-6

LANE = 128
HEAD_PAD = LANE
M_PAD = M_HEADS * HEAD_PAD
A_PAD = A_HEADS * HEAD_PAD
VMEM_LIMIT = 48 * 1024 * 1024

_COL_QK = (0, 2 * M_PAD)
_COL_V = (_COL_QK[1], _COL_QK[1] + M_PAD)
_COL_O = (_COL_V[1], _COL_V[1] + M_PAD)
_COL_G = (_COL_O[1], _COL_O[1] + LANE)
_COL_CQ = (_COL_G[1], _COL_G[1] + A_Q_RANK)
_COL_CKR = (_COL_CQ[1], _COL_CQ[1] + 2 * LANE)
_COL_CU = (_COL_CKR[1], _COL_CKR[1] + C_WIDTH)
_COL_CV = (_COL_CU[1], _COL_CU[1] + C_WIDTH)
D_IN_PAD = _COL_CV[1]

ATT_BLOCK = 256
MOE_BLOCK = 256
NEG = -1e30

_BF = jnp.bfloat16
_F32 = jnp.float32


def _dot(a, b):
    return jnp.dot(a, b, preferred_element_type=_F32)


def _dot_nt(a, b):
    return lax.dot_general(a, b, (((1,), (1,)), ((), ())), preferred_element_type=_F32)


def _dot_tn(a, b):
    return lax.dot_general(a, b, (((0,), (0,)), ((), ())), preferred_element_type=_F32)


def _split2(a):
    hi = a.astype(_BF)
    lo = (a - hi.astype(_F32)).astype(_BF)
    return hi, lo


def _split3(a):
    hi = a.astype(_BF)
    r1 = a - hi.astype(_F32)
    mid = r1.astype(_BF)
    lo = (r1 - mid.astype(_F32)).astype(_BF)
    return hi, mid, lo


def _group_sum(a, ones_bd):
    hi, lo = _split2(a)
    return _dot(hi, ones_bd) + _dot(lo, ones_bd)


def _params(n_axes):
    return pltpu.CompilerParams(dimension_semantics=("arbitrary",) * n_axes, vmem_limit_bytes=VMEM_LIMIT)


def _const_spec(shape):
    nd = len(shape)
    return pl.BlockSpec(shape, lambda *_: (0,) * nd)


def _token_tile(T):
    for tm in (512, 256, 128):
        if T % tm == 0:
            return tm
    raise ValueError(f"token count {T} must be a multiple of 128")


def _rope_kernel(pos_ref, invf_ref, sgn_ref, out_ref):
    S = pos_ref.shape[-1]
    pos = pos_ref[0].astype(_F32)
    ang = invf_ref[...] * pos
    c = jnp.cos(ang)
    s = jnp.sin(ang) * sgn_ref[...]
    for j in range(S // LANE):
        out_ref[0, j * LANE:(j + 1) * LANE, 0:LANE] = c[:, j * LANE:(j + 1) * LANE].T
        out_ref[0, j * LANE:(j + 1) * LANE, LANE:2 * LANE] = s[:, j * LANE:(j + 1) * LANE].T


def _rope_tables(positions):
    B, S = positions.shape
    inv_freq = 1.0 / (ROPE_THETA ** (jnp.arange(0, A_ROPE, 2, dtype=_F32) / A_ROPE))
    half = A_ROPE // 2
    invf = jnp.zeros((LANE,), _F32).at[A_NOPE:A_NOPE + A_ROPE].set(jnp.concatenate([inv_freq, inv_freq]))
    sgn = jnp.zeros((LANE,), _F32).at[A_NOPE:A_NOPE + half].set(-1.0).at[A_NOPE + half:A_NOPE + A_ROPE].set(1.0)
    return pl.pallas_call(
        _rope_kernel,
        grid=(B,),
        in_specs=[pl.BlockSpec((1, 1, S), lambda b: (b, 0, 0)), _const_spec((LANE, 1)), _const_spec((LANE, 1))],
        out_specs=pl.BlockSpec((1, S, 2 * LANE), lambda b: (b, 0, 0)),
        out_shape=jax.ShapeDtypeStruct((B, S, 2 * LANE), _F32),
        compiler_params=_params(1),
        name="rope_tables",
    )(positions.reshape(B, 1, S), invf.reshape(LANE, 1), sgn.reshape(LANE, 1)).reshape(B * S, 2 * LANE)


def _inproj_kernel(x_ref, g_ref, w_ref, cvg_ref, ws_ref, bs_ref, cg_ref, ones_ref,
                   qk_ref, v_ref, o_ref, gate_ref, cq_ref, ckr_ref, yc_ref):
    tm = x_ref.shape[0]
    x = x_ref[...]
    h = (x * lax.rsqrt(jnp.mean(x * x, axis=-1, keepdims=True) + EPS) * g_ref[...]).astype(_BF)

    def proj(col):
        return _dot(h, w_ref[:, col[0]:col[1]])

    qk_ref[...] = proj(_COL_QK).astype(_BF)
    v_ref[...] = proj(_COL_V).astype(_BF)
    o_ref[...] = proj(_COL_O).astype(_BF)
    gate_ref[...] = proj(_COL_G)
    cq_ref[...] = proj(_COL_CQ).astype(_BF)
    ckr_ref[...] = proj(_COL_CKR).astype(_BF)

    ones_bd = ones_ref[...]
    u = jax.nn.gelu(proj(_COL_CU))
    v = jax.nn.gelu(proj(_COL_CV))
    vn = v * lax.rsqrt(_group_sum(v * v, ones_bd) * (1.0 / C_GROUP_DIM) + EPS) * cvg_ref[...]
    vnb = vn.astype(_BF)
    lane = lax.broadcasted_iota(jnp.int32, (C_CHUNK, C_WIDTH), 1)
    for c in range(tm // C_CHUNK):
        rows = slice(c * C_CHUNK, (c + 1) * C_CHUNK)
        vc = vnb[rows]
        mixed = _dot(ws_ref[0], vc)
        for g in range(1, C_GROUPS):
            mixed = jnp.where(lane >= g * C_GROUP_DIM, _dot(ws_ref[g], vc), mixed)
        hc = u[rows] * (mixed + bs_ref[...])
        y = hc * lax.rsqrt(_group_sum(hc * hc, ones_bd) * (1.0 / C_GROUP_DIM) + EPS) * cg_ref[...]
        yc_ref[rows, :] = y.astype(_BF)


def _inproj(x, lw, tm):
    T = x.shape[0]

    def tok(width):
        return pl.BlockSpec((tm, width), lambda i: (i, 0))

    outs = [(2 * M_PAD, _BF), (M_PAD, _BF), (M_PAD, _BF), (LANE, _F32), (A_Q_RANK, _BF), (2 * LANE, _BF), (C_WIDTH, _BF)]
    return pl.pallas_call(
        _inproj_kernel,
        grid=(T // tm,),
        in_specs=[tok(D_MODEL), _const_spec((1, D_MODEL)), _const_spec((D_MODEL, D_IN_PAD)),
                  _const_spec((1, C_WIDTH)), _const_spec((C_GROUPS, C_CHUNK, C_CHUNK)),
                  _const_spec((C_CHUNK, C_WIDTH)), _const_spec((1, C_WIDTH)), _const_spec((C_WIDTH, C_WIDTH))],
        out_specs=[tok(w) for w, _ in outs],
        out_shape=[jax.ShapeDtypeStruct((T, w), dt) for w, dt in outs],
        compiler_params=_params(1),
        name="inproj_gating",
    )(x, lw["attn_norm"], lw["w_in"], lw["c_v_norm"], lw["c_w_s"], lw["c_b_s"], lw["c_out_norm"], lw["ones_bd"])


def _mlstm_kernel(qk_ref, v_ref, o_ref, gate_ref, cw_ref, cb_ref, gb_ref, ng_ref, tri_ref,
                  y_ref, ct_ref, n_ref, m_ref):
    S = qk_ref.shape[0]
    L = M_CHUNK
    ct_ref[...] = jnp.zeros_like(ct_ref)
    n_ref[...] = jnp.zeros_like(n_ref)
    m_ref[...] = jnp.zeros_like(m_ref)
    lane = lax.broadcasted_iota(jnp.int32, (L, LANE), 1)
    causal = lax.broadcasted_iota(jnp.int32, (L, L), 0) >= lax.broadcasted_iota(jnp.int32, (L, L), 1)
    k_scale = M_HEAD_DIM ** -0.5
    halo = 16

    def chunk(c, carry):
        r0 = pl.multiple_of(c * L, L)
        cur = qk_ref[pl.ds(r0, L), :].astype(_F32)
        p0 = pl.multiple_of(jnp.maximum(r0 - halo, 0), halo)
        prev = jnp.where(c > 0, qk_ref[pl.ds(p0, halo), :].astype(_F32), 0.0)
        ext = jnp.concatenate([prev, cur], axis=0)
        conv = cb_ref[...] + ext[halo:] * cw_ref[CONV_WIDTH - 1:CONV_WIDTH, :]
        for j in range(CONV_WIDTH - 1):
            shift = CONV_WIDTH - 1 - j
            conv = conv + pltpu.roll(ext, shift, 0)[halo:] * cw_ref[j:j + 1, :]
        qk = conv * jax.nn.sigmoid(conv)

        G = gate_ref[pl.ds(r0, L), :] + gb_ref[...]
        logf = jnp.minimum(G, 0.0) - jnp.log1p(jnp.exp(-jnp.abs(G)))
        Z = jnp.where(lane < M_HEADS, G, jnp.where(lane < 2 * M_HEADS, logf, 0.0))
        tri = tri_ref[...]
        zh, zm, zl = _split3(Z)
        cum = _dot(tri, zh) + _dot(tri, zm) + _dot(tri, zl)
        ZT = Z.T
        cumT = cum.T

        for h in range(M_HEADS):
            hs = slice(h * HEAD_PAD, (h + 1) * HEAD_PAD)
            q = qk[:, h * HEAD_PAD:(h + 1) * HEAD_PAD]
            k = qk[:, M_PAD + h * HEAD_PAD:M_PAD + (h + 1) * HEAD_PAD] * k_scale
            v = v_ref[pl.ds(r0, L), hs]
            qb = q.astype(_BF)
            kb = k.astype(_BF)
            b_col = cum[:, M_HEADS + h:M_HEADS + h + 1]
            i_col = Z[:, h:h + 1]
            b_row = cumT[M_HEADS + h:M_HEADS + h + 1, :]
            i_row = ZT[h:h + 1, :]
            m_prev = m_ref[h][:, 0:1]
            ct = ct_ref[h]
            n = n_ref[h]

            d = jnp.where(causal, b_col - b_row + i_row, NEG)
            inter = b_col + m_prev
            m_row = jnp.maximum(inter, jnp.max(d, axis=-1, keepdims=True))
            w_intra = jnp.exp(d - m_row)
            w_inter = jnp.exp(inter - m_row)
            s = _dot_nt(qb, kb) * w_intra
            num = _dot(s.astype(_BF), v) + w_inter * _dot(qb, ct.astype(_BF))
            den = jnp.sum(s, axis=-1, keepdims=True) + w_inter * jnp.sum(q * n, axis=-1, keepdims=True)
            hval = num / jnp.maximum(jnp.abs(den), jnp.exp(-m_row))

            b_last = b_col[L - 1:L, :]
            g_row = b_last - b_row + i_row
            g_col = b_last - b_col + i_col
            m_new = jnp.maximum(b_last + m_prev, jnp.max(g_row, axis=-1, keepdims=True))
            a = jnp.exp(b_last + m_prev - m_new)
            wg = jnp.exp(g_col - m_new)
            ct_ref[h] = a * ct + _dot_tn(kb, (wg * v.astype(_F32)).astype(_BF))
            n_ref[h] = a * n + jnp.sum(wg * k, axis=0, keepdims=True)
            m_ref[h] = jnp.broadcast_to(m_new, (1, LANE))

            hm = jax.nn.sigmoid(o_ref[pl.ds(r0, L), hs].astype(_F32)) * hval
            ms = jnp.sum(hm * hm, axis=-1, keepdims=True) * (1.0 / M_HEAD_DIM)
            y_ref[pl.ds(r0, L), hs] = (hm * lax.rsqrt(ms + EPS) * ng_ref[:, hs]).astype(_BF)
        return carry

    lax.fori_loop(0, S // L, chunk, 0)


def _mlstm(qk, v, o, gates, lw, B, S):
    T = B * S

    def seq(width):
        return pl.BlockSpec((S, width), lambda b: (b, 0))

    return pl.pallas_call(
        _mlstm_kernel,
        grid=(B,),
        in_specs=[seq(2 * M_PAD), seq(M_PAD), seq(M_PAD), seq(LANE),
                  _const_spec((CONV_WIDTH, 2 * M_PAD)), _const_spec((1, 2 * M_PAD)), _const_spec((1, LANE)),
                  _const_spec((1, M_PAD)), _const_spec((M_CHUNK, M_CHUNK))],
        out_specs=seq(M_PAD),
        out_shape=jax.ShapeDtypeStruct((T, M_PAD), _BF),
        scratch_shapes=[pltpu.VMEM((M_HEADS, HEAD_PAD, HEAD_PAD), _F32),
                        pltpu.VMEM((M_HEADS, 1, LANE), _F32),
                        pltpu.VMEM((M_HEADS, 1, LANE), _F32)],
        compiler_params=_params(1),
        name="mlstm",
    )(qk, v, o, gates, lw["m_conv_w"], lw["m_conv_b"], lw["m_gate_bias"], lw["m_out_norm"], lw["tri_incl"])


def _mla_kernel(cq_ref, ckr_ref, rope_ref, qg_ref, kvg_ref, wuq_ref, wukv_ref, qhg_ref, khg_ref, ag_ref,
                y_ref, k_scr, v_scr):
    S = ckr_ref.shape[0]
    BQ = ATT_BLOCK
    qi = pl.program_id(1)
    lane = lax.broadcasted_iota(jnp.int32, (BQ, LANE), 1)
    swap_lo = lane < A_NOPE + A_ROPE // 2
    inv_qk = 1.0 / A_QK_DIM

    def rope(xn, rt, sg):
        sw = jnp.where(swap_lo, pltpu.roll(xn, LANE - A_ROPE // 2, 1), pltpu.roll(xn, A_ROPE // 2, 1))
        return xn * rt + sw * sg

    @pl.when(qi == 0)
    def _build_kv():
        def kv_block(j, carry):
            r0 = pl.multiple_of(j * BQ, BQ)
            ck = ckr_ref[pl.ds(r0, BQ), :].astype(_F32)
            ckv = ck[:, :LANE]
            k_rope = ck[:, LANE:]
            kvn = (ckv * lax.rsqrt(jnp.mean(ckv * ckv, axis=-1, keepdims=True) + EPS) * kvg_ref[...]).astype(_BF)
            kv = _dot(kvn, wukv_ref[...])
            v_scr[pl.ds(r0, BQ), :] = kv[:, A_PAD:].astype(_BF)
            rt = rope_ref[pl.ds(r0, BQ), 0:LANE]
            sg = rope_ref[pl.ds(r0, BQ), LANE:2 * LANE]
            for h in range(A_HEADS):
                hs = slice(h * HEAD_PAD, (h + 1) * HEAD_PAD)
                kh = kv[:, hs] + k_rope
                r = lax.rsqrt(jnp.sum(kh * kh, axis=-1, keepdims=True) * inv_qk + EPS)
                k_scr[pl.ds(r0, BQ), hs] = rope(kh * r * khg_ref[...], rt, sg).astype(_BF)
            return carry

        lax.fori_loop(0, S // BQ, kv_block, 0)

    q0 = pl.multiple_of(qi * BQ, BQ)
    cq = cq_ref[...].astype(_F32)
    qn = (cq * lax.rsqrt(jnp.mean(cq * cq, axis=-1, keepdims=True) + EPS) * qg_ref[...]).astype(_BF)
    qall = _dot(qn, wuq_ref[...])
    rt = rope_ref[pl.ds(q0, BQ), 0:LANE]
    sg = rope_ref[pl.ds(q0, BQ), LANE:2 * LANE]
    causal = lax.broadcasted_iota(jnp.int32, (BQ, BQ), 0) >= lax.broadcasted_iota(jnp.int32, (BQ, BQ), 1)
    scale = A_QK_DIM ** -0.5

    for p in range(A_HEADS // 2):
        ps = slice(p * LANE, (p + 1) * LANE)
        outs = []
        for hh in range(2):
            h = 2 * p + hh
            hs = slice(h * HEAD_PAD, (h + 1) * HEAD_PAD)
            qh = qall[:, hs]
            r = lax.rsqrt(jnp.sum(qh * qh, axis=-1, keepdims=True) * inv_qk + EPS)
            qb = (rope(qh * r * qhg_ref[...], rt, sg) * scale).astype(_BF)

            def update(s, vb, m, l, acc):
                m_new = jnp.maximum(m, jnp.max(s, axis=-1, keepdims=True))
                alpha = jnp.exp(m - m_new)
                pe = jnp.exp(s - m_new)
                l = alpha * l + jnp.sum(pe, axis=-1, keepdims=True)
                acc = alpha * acc + _dot(pe.astype(_BF), vb)
                return m_new, l, acc

            def kblock(j, carry, qb=qb, hs=hs, ps=ps):
                k0 = pl.multiple_of(j * BQ, BQ)
                s = _dot_nt(qb, k_scr[pl.ds(k0, BQ), hs])
                return update(s, v_scr[pl.ds(k0, BQ), ps], *carry)

            init = (jnp.full((BQ, 1), NEG, _F32), jnp.zeros((BQ, 1), _F32), jnp.zeros((BQ, LANE), _F32))
            m, l, acc = lax.fori_loop(0, qi, kblock, init)
            s = jnp.where(causal, _dot_nt(qb, k_scr[pl.ds(q0, BQ), hs]), NEG)
            m, l, acc = update(s, v_scr[pl.ds(q0, BQ), ps], m, l, acc)
            outs.append(acc / l)
        lo = lane < A_V_DIM
        o = jnp.where(lo, outs[0], outs[1])
        o2 = o * o
        ms_lo = jnp.sum(jnp.where(lo, o2, 0.0), axis=-1, keepdims=True) * (1.0 / A_V_DIM)
        ms_hi = jnp.sum(jnp.where(lo, 0.0, o2), axis=-1, keepdims=True) * (1.0 / A_V_DIM)
        rs = jnp.where(lo, lax.rsqrt(ms_lo + EPS), lax.rsqrt(ms_hi + EPS))
        y_ref[:, ps] = (o * rs * ag_ref[:, ps]).astype(_BF)


def _mla(cq, ckr, rope, lw, B, S):
    T = B * S
    BQ = ATT_BLOCK
    nq = S // BQ
    return pl.pallas_call(
        _mla_kernel,
        grid=(B, nq),
        in_specs=[pl.BlockSpec((BQ, A_Q_RANK), lambda b, q: (b * nq + q, 0)),
                  pl.BlockSpec((S, 2 * LANE), lambda b, q: (b, 0)),
                  pl.BlockSpec((S, 2 * LANE), lambda b, q: (b, 0)),
                  _const_spec((1, A_Q_RANK)), _const_spec((1, A_KV_RANK)),
                  _const_spec((A_Q_RANK, A_PAD)), _const_spec((A_KV_RANK, A_PAD + A_WIDTH)),
                  _const_spec((1, LANE)), _const_spec((1, LANE)), _const_spec((1, A_WIDTH))],
        out_specs=pl.BlockSpec((BQ, A_WIDTH), lambda b, q: (b * nq + q, 0)),
        out_shape=jax.ShapeDtypeStruct((T, A_WIDTH), _BF),
        scratch_shapes=[pltpu.VMEM((S, A_PAD), _BF), pltpu.VMEM((S, A_WIDTH), _BF)],
        compiler_params=_params(2),
        name="mla",
    )(cq, ckr, rope, lw["a_q_norm"], lw["a_kv_norm"], lw["a_w_uq"], lw["a_w_ukv"],
      lw["a_q_head_norm"], lw["a_k_head_norm"], lw["a_out_norm"])


def _outproj_router_kernel(ym_ref, ya_ref, yc_ref, x_ref, wm_ref, wa_ref, wc_ref, g_ref, wr_hi_ref, wr_lo_ref,
                           br_ref, tri_ref, xmid_ref, xn_ref, ri_ref, rw_ref, cnt_ref):
    tm = x_ref.shape[0]
    i = pl.program_id(0)

    @pl.when(i == 0)
    def _():
        cnt_ref[...] = jnp.zeros_like(cnt_ref)

    y = _dot(ym_ref[...], wm_ref[...]) + _dot(ya_ref[...], wa_ref[...]) + _dot(yc_ref[...], wc_ref[...])
    xm = x_ref[...] + y
    xmid_ref[...] = xm
    xn = xm * lax.rsqrt(jnp.mean(xm * xm, axis=-1, keepdims=True) + EPS) * g_ref[...]
    xn_ref[...] = xn

    xh, xl = _split2(xn)
    logits = _dot(xh, wr_hi_ref[...]) + _dot(xh, wr_lo_ref[...]) + _dot(xl, wr_hi_ref[...])
    biased = logits + br_ref[...]
    lane = lax.broadcasted_iota(jnp.int32, (tm, LANE), 1)
    lane_f = lane.astype(_F32)

    def lane_max(a):
        return jnp.max(a, axis=-1, keepdims=True)

    def lane_sum(a):
        return jnp.sum(a, axis=-1, keepdims=True)

    def first_argmax(a, amax):
        return jnp.min(jnp.where(a == amax, lane_f, float(LANE)), axis=-1, keepdims=True).astype(jnp.int32)

    gmask = lane < N_GROUPS
    gl = jnp.where(gmask, logits, NEG)
    gexp = jnp.where(gmask, jnp.exp(gl - lane_max(gl)), 0.0)
    gsc = jnp.where(gmask, biased, NEG)
    g_sel = first_argmax(gsc, lane_max(gsc))
    g_gate = lane_sum(jnp.where(lane == g_sel, gexp, 0.0)) / lane_sum(gexp)

    e_lo = N_GROUPS + EXPERTS_PER_GROUP * g_sel
    emask = jnp.logical_and(lane >= e_lo, lane < e_lo + EXPERTS_PER_GROUP)
    el = jnp.where(emask, logits, NEG)
    eexp = jnp.where(emask, jnp.exp(el - lane_max(el)), 0.0)
    esc = jnp.where(emask, biased, NEG)
    i1 = first_argmax(esc, lane_max(esc))
    esc2 = jnp.where(lane == i1, NEG, esc)
    i2 = first_argmax(esc2, lane_max(esc2))
    oh1 = lane == i1
    oh2 = lane == i2
    p1 = lane_sum(jnp.where(oh1, eexp, 0.0))
    p2 = lane_sum(jnp.where(oh2, eexp, 0.0))
    w1 = g_gate * p1 / (p1 + p2)
    w2 = g_gate * p2 / (p1 + p2)

    cnt = jnp.where(oh1, 1.0, 0.0) + jnp.where(oh2, 1.0, 0.0)
    base = cnt_ref[...] + _dot(tri_ref[...], cnt.astype(_BF))
    rank1 = lane_sum(jnp.where(oh1, base, 0.0)).astype(jnp.int32)
    rank2 = lane_sum(jnp.where(oh2, base, 0.0)).astype(jnp.int32)
    cnt_ref[...] = cnt_ref[...] + jnp.sum(cnt, axis=0, keepdims=True)

    ri = jnp.where(lane == 0, i1 - N_GROUPS, jnp.where(lane == 1, i2 - N_GROUPS,
                                                       jnp.where(lane == 2, rank1, jnp.where(lane == 3, rank2, 0))))
    ri_ref[...] = ri
    rw_ref[...] = jnp.where(lane == 0, w1, jnp.where(lane == 1, w2, 0.0))


def _outproj_router(ym, ya, yc, x, lw, tm):
    T = x.shape[0]

    def tok(width):
        return pl.BlockSpec((tm, width), lambda i: (i, 0))

    return pl.pallas_call(
        _outproj_router_kernel,
        grid=(T // tm,),
        in_specs=[tok(M_PAD), tok(A_WIDTH), tok(C_WIDTH), tok(D_MODEL),
                  _const_spec((M_PAD, D_MODEL)), _const_spec((A_WIDTH, D_MODEL)), _const_spec((C_WIDTH, D_MODEL)),
                  _const_spec((1, D_MODEL)), _const_spec((D_MODEL, LANE)), _const_spec((D_MODEL, LANE)),
                  _const_spec((1, LANE)), _const_spec((tm, tm))],
        out_specs=[tok(D_MODEL), tok(D_MODEL), tok(LANE), tok(LANE), _const_spec((1, LANE))],
        out_shape=[jax.ShapeDtypeStruct((T, D_MODEL), _F32), jax.ShapeDtypeStruct((T, D_MODEL), _F32),
                   jax.ShapeDtypeStruct((T, LANE), jnp.int32), jax.ShapeDtypeStruct((T, LANE), _F32),
                   jax.ShapeDtypeStruct((1, LANE), _F32)],
        compiler_params=_params(1),
        name="outproj_router",
    )(ym, ya, yc, x, lw["w_out_m"], lw["w_out_a"], lw["w_out_c"], lw["ffn_norm"], lw["w_router_hi"],
      lw["w_router_lo"], lw["b_router"], lw["tri_strict"])


def _row_copy(src, s_row, dst, d_row, sem):
    return pltpu.make_async_copy(src.at[pl.ds(s_row, 1)], dst.at[pl.ds(d_row, 1)], sem)


def _dispatch_kernel(dest_ref, xn_ref, xs_hbm, sem):
    tm = xn_ref.shape[0]

    def issue(t, carry):
        for k in range(2):
            _row_copy(xn_ref, t, xs_hbm, dest_ref[0, 0, 2 * t + k], sem).start()
        return carry

    def drain(t, carry):
        for k in range(2):
            _row_copy(xn_ref, t, xs_hbm, dest_ref[0, 0, 2 * t + k], sem).wait()
        return carry

    lax.fori_loop(0, tm, issue, 0)
    lax.fori_loop(0, tm, drain, 0)


def _dispatch(xn, dest, tm):
    T = xn.shape[0]
    return pl.pallas_call(
        _dispatch_kernel,
        grid=(T // tm,),
        in_specs=[pl.BlockSpec((1, 1, 2 * tm), lambda i: (i, 0, 0), memory_space=pltpu.SMEM),
                  pl.BlockSpec((tm, D_MODEL), lambda i: (i, 0))],
        out_specs=pl.BlockSpec(memory_space=pl.ANY),
        out_shape=jax.ShapeDtypeStruct((2 * T, D_MODEL), _F32),
        scratch_shapes=[pltpu.SemaphoreType.DMA],
        compiler_params=pltpu.CompilerParams(dimension_semantics=("arbitrary",), vmem_limit_bytes=VMEM_LIMIT,
                                             has_side_effects=True),
        name="moe_dispatch",
    )(dest.reshape(T // tm, 1, 2 * tm), xn)


def _expert_kernel(vblk_ref, vexp_ref, vok_ref, off_ref, xs_ref, w1_ref, w3_ref, w2_ref, out_ref):
    v = pl.program_id(0)
    blk = vblk_ref[v]
    e = vexp_ref[v]
    first = jnp.logical_or(v == 0, vblk_ref[jnp.maximum(v - 1, 0)] != blk)

    @pl.when(vok_ref[v] != 0)
    def _():
        xb = xs_ref[...].astype(_BF)
        hid = (jax.nn.silu(_dot(xb, w1_ref[0])) * _dot(xb, w3_ref[0])).astype(_BF)
        y = _dot(hid, w2_ref[0])
        rows = blk * MOE_BLOCK + lax.broadcasted_iota(jnp.int32, (MOE_BLOCK, 1), 0)
        y = jnp.where(jnp.logical_and(rows >= off_ref[e], rows < off_ref[e + 1]), y, 0.0)

        @pl.when(first)
        def _():
            out_ref[...] = y

        @pl.when(jnp.logical_not(first))
        def _():
            out_ref[...] += y


def _experts(xs, vblk, vexp, vok, offs, lw):
    A = xs.shape[0]
    n_visits = vblk.shape[0]
    grid_spec = pltpu.PrefetchScalarGridSpec(
        num_scalar_prefetch=4,
        grid=(n_visits,),
        in_specs=[pl.BlockSpec((MOE_BLOCK, D_MODEL), lambda v, vb, ve, vo, of: (vb[v], 0)),
                  pl.BlockSpec((1, D_MODEL, D_EXPERT), lambda v, vb, ve, vo, of: (ve[v], 0, 0)),
                  pl.BlockSpec((1, D_MODEL, D_EXPERT), lambda v, vb, ve, vo, of: (ve[v], 0, 0)),
                  pl.BlockSpec((1, D_EXPERT, D_MODEL), lambda v, vb, ve, vo, of: (ve[v], 0, 0))],
        out_specs=pl.BlockSpec((MOE_BLOCK, D_MODEL), lambda v, vb, ve, vo, of: (vb[v], 0)),
    )
    return pl.pallas_call(
        _expert_kernel,
        grid_spec=grid_spec,
        out_shape=jax.ShapeDtypeStruct((A, D_MODEL), _F32),
        compiler_params=_params(1),
        name="moe_experts",
    )(vblk, vexp, vok, offs, xs, lw["w1"], lw["w3"], lw["w2"])


def _combine_kernel(dest_ref, xmid_ref, rw_ref, yb_hbm, out_ref, buf, sem):
    tm = xmid_ref.shape[0]

    def issue(t, carry):
        for k in range(2):
            _row_copy(yb_hbm, dest_ref[0, 0, 2 * t + k], buf.at[k], t, sem).start()
        return carry

    def drain(t, carry):
        for k in range(2):
            _row_copy(yb_hbm, dest_ref[0, 0, 2 * t + k], buf.at[k], t, sem).wait()
        return carry

    lax.fori_loop(0, tm, issue, 0)
    lax.fori_loop(0, tm, drain, 0)
    w = rw_ref[...]
    out_ref[...] = xmid_ref[...] + w[:, 0:1] * buf[0] + w[:, 1:2] * buf[1]


def _combine(xmid, rw, yb, dest, tm):
    T = xmid.shape[0]
    return pl.pallas_call(
        _combine_kernel,
        grid=(T // tm,),
        in_specs=[pl.BlockSpec((1, 1, 2 * tm), lambda i: (i, 0, 0), memory_space=pltpu.SMEM),
                  pl.BlockSpec((tm, D_MODEL), lambda i: (i, 0)),
                  pl.BlockSpec((tm, LANE), lambda i: (i, 0)),
                  pl.BlockSpec(memory_space=pl.ANY)],
        out_specs=pl.BlockSpec((tm, D_MODEL), lambda i: (i, 0)),
        out_shape=jax.ShapeDtypeStruct((T, D_MODEL), _F32),
        scratch_shapes=[pltpu.VMEM((2, tm, D_MODEL), _F32), pltpu.SemaphoreType.DMA],
        compiler_params=_params(1),
        name="moe_combine",
    )(dest.reshape(T // tm, 1, 2 * tm), xmid, rw, yb)


def _routing_plan(ri, counts, T):
    A = 2 * T
    n_blocks = A // MOE_BLOCK
    n_visits = n_blocks + N_EXPERTS - 1
    cnt = counts[0, N_GROUPS:N_GROUPS + N_EXPERTS].astype(jnp.int32)
    ends = jnp.cumsum(cnt)
    starts = ends - cnt
    offs = jnp.concatenate([starts, ends[-1:]])
    dest = (starts[ri[:, 0:2]] + ri[:, 2:4]).reshape(A)
    first_blk = starts // MOE_BLOCK
    n_vis = jnp.where(cnt > 0, (ends - 1) // MOE_BLOCK - first_blk + 1, 0)
    vis_end = jnp.cumsum(n_vis)
    vis_start = vis_end - n_vis
    total = vis_end[-1]
    vid = jnp.arange(n_visits, dtype=jnp.int32)
    vclamped = jnp.minimum(vid, total - 1)
    vexp = jnp.sum(vis_end[None, :] <= vclamped[:, None], axis=1).astype(jnp.int32)
    vblk = (first_blk[vexp] + vclamped - vis_start[vexp]).astype(jnp.int32)
    vok = (vid < total).astype(jnp.int32)
    return dest.astype(jnp.int32), vblk, vexp, vok, offs.astype(jnp.int32)


def _pad_heads(w, n_heads, dim):
    lead = w.shape[:-1]
    w = w.reshape(lead + (n_heads, dim))
    w = jnp.pad(w, [(0, 0)] * len(lead) + [(0, 0), (0, HEAD_PAD - dim)])
    return w.reshape(lead + (n_heads * HEAD_PAD,))


def _pack_layer(p, l, tm):
    w_in = p["w_in"][l]
    sizes = (M_WIDTH, M_WIDTH, M_WIDTH, M_WIDTH, M_HEADS, M_HEADS, A_Q_RANK, A_KV_RANK, A_ROPE, C_WIDTH, C_WIDTH)
    cols, off = [], 0
    for s in sizes:
        cols.append(w_in[:, off:off + s])
        off += s
    mq, mk, mv, mo, mi, mf, cq, ckv, krope, cu, cv = cols
    zeros = lambda n: jnp.zeros((D_MODEL, n), _F32)
    w_all = jnp.concatenate([
        _pad_heads(mq, M_HEADS, M_HEAD_DIM), _pad_heads(mk, M_HEADS, M_HEAD_DIM),
        _pad_heads(mv, M_HEADS, M_HEAD_DIM), _pad_heads(mo, M_HEADS, M_HEAD_DIM),
        mi, mf, zeros(LANE - 2 * M_HEADS),
        cq, ckv, zeros(A_NOPE), krope, zeros(LANE - A_NOPE - A_ROPE), cu, cv], axis=1).astype(_BF)

    g = p["mix_out_norm"][l]
    w_out = p["w_out"][l]
    w_out_m = jnp.pad(w_out[:M_WIDTH].reshape(M_HEADS, M_HEAD_DIM, D_MODEL),
                      ((0, 0), (0, HEAD_PAD - M_HEAD_DIM), (0, 0))).reshape(M_PAD, D_MODEL)

    w_uq = _pad_heads(p["a_w_uq"][l], A_HEADS, A_QK_DIM)
    w_ukv = p["a_w_ukv"][l].reshape(A_KV_RANK, A_HEADS, A_NOPE + A_V_DIM)
    w_uk = _pad_heads(w_ukv[:, :, :A_NOPE].reshape(A_KV_RANK, A_HEADS * A_NOPE), A_HEADS, A_NOPE)
    w_uv = w_ukv[:, :, A_NOPE:].reshape(A_KV_RANK, A_WIDTH)

    w_router = jnp.concatenate([p["w_group"][l], p["w_expert"][l],
                                jnp.zeros((D_MODEL, LANE - N_GROUPS - N_EXPERTS), _F32)], axis=1)
    wr_hi = w_router.astype(_BF)
    wr_lo = (w_router - wr_hi.astype(_F32)).astype(_BF)
    b_router = jnp.concatenate([p["b_group"][l], p["b_expert"][l],
                                jnp.zeros((LANE - N_GROUPS - N_EXPERTS,), _F32)]).reshape(1, LANE)

    tril = jnp.tril(jnp.ones((C_CHUNK, C_CHUNK), _F32))
    lane_pad = lambda a: jnp.pad(a, (0, HEAD_PAD - a.shape[0])).reshape(1, HEAD_PAD)
    return {
        "attn_norm": p["attn_norm"][l].reshape(1, D_MODEL),
        "w_in": w_all,
        "c_v_norm": p["c_v_norm"][l].reshape(1, C_WIDTH),
        "c_w_s": (p["c_w_s"][l] * tril).astype(_BF),
        "c_b_s": jnp.repeat(p["c_b_s"][l].T, C_GROUP_DIM, axis=1),
        "c_out_norm": g[M_WIDTH + A_WIDTH:].reshape(1, C_WIDTH),
        "ones_bd": jnp.kron(jnp.eye(C_GROUPS, dtype=_F32), jnp.ones((C_GROUP_DIM, C_GROUP_DIM), _F32)).astype(_BF),
        "m_conv_w": jnp.concatenate([_pad_heads(p["m_conv_w"][l][:, :M_WIDTH], M_HEADS, M_HEAD_DIM),
                                     _pad_heads(p["m_conv_w"][l][:, M_WIDTH:], M_HEADS, M_HEAD_DIM)], axis=1),
        "m_conv_b": jnp.concatenate([_pad_heads(p["m_conv_b"][l][:M_WIDTH], M_HEADS, M_HEAD_DIM),
                                     _pad_heads(p["m_conv_b"][l][M_WIDTH:], M_HEADS, M_HEAD_DIM)]).reshape(1, 2 * M_PAD),
        "m_gate_bias": jnp.pad(p["m_gate_bias"][l], (0, LANE - 2 * M_HEADS)).reshape(1, LANE),
        "m_out_norm": _pad_heads(g[:M_WIDTH], M_HEADS, M_HEAD_DIM).reshape(1, M_PAD),
        "tri_incl": tril.astype(_BF),
        "a_q_norm": p["a_q_norm"][l].reshape(1, A_Q_RANK),
        "a_kv_norm": p["a_kv_norm"][l].reshape(1, A_KV_RANK),
        "a_w_uq": w_uq.astype(_BF),
        "a_w_ukv": jnp.concatenate([w_uk, w_uv], axis=1).astype(_BF),
        "a_q_head_norm": lane_pad(p["a_q_head_norm"][l]),
        "a_k_head_norm": lane_pad(p["a_k_head_norm"][l]),
        "a_out_norm": g[M_WIDTH:M_WIDTH + A_WIDTH].reshape(1, A_WIDTH),
        "w_out_m": w_out_m.astype(_BF),
        "w_out_a": w_out[M_WIDTH:M_WIDTH + A_WIDTH].astype(_BF),
        "w_out_c": w_out[M_WIDTH + A_WIDTH:].astype(_BF),
        "ffn_norm": p["ffn_norm"][l].reshape(1, D_MODEL),
        "w_router_hi": wr_hi,
        "w_router_lo": wr_lo,
        "b_router": b_router,
        "tri_strict": jnp.tril(jnp.ones((tm, tm), _F32), -1).astype(_BF),
        "w1": p["w1"][l].astype(_BF),
        "w3": p["w3"][l].astype(_BF),
        "w2": p["w2"][l].astype(_BF),
    }


def _layer(x, rope, lw, B, S, tm):
    T = B * S
    qk, v, o, gates, cq, ckr, yc = _inproj(x, lw, tm)
    ym = _mlstm(qk, v, o, gates, lw, B, S)
    ya = _mla(cq, ckr, rope, lw, B, S)
    xmid, xn, ri, rw, counts = _outproj_router(ym, ya, yc, x, lw, tm)
    dest, vblk, vexp, vok, offs = _routing_plan(ri, counts, T)
    xs = _dispatch(xn, dest, tm)
    yb = _experts(xs, vblk, vexp, vok, offs, lw)
    return _combine(xmid, rw, yb, dest, tm)


def kernel(x, positions, attn_norm, w_in, m_conv_w, m_conv_b, m_gate_bias, a_q_norm, a_kv_norm, a_w_uq, a_w_ukv, a_q_head_norm, a_k_head_norm, c_v_norm, c_w_s, c_b_s, mix_out_norm, w_out, ffn_norm, w_group, b_group, w_expert, b_expert, w1, w3, w2):
    B, S, D = x.shape
    assert D == D_MODEL and S % ATT_BLOCK == 0 and (2 * B * S) % MOE_BLOCK == 0
    T = B * S
    tm = _token_tile(T)
    p = dict(attn_norm=attn_norm, w_in=w_in, m_conv_w=m_conv_w, m_conv_b=m_conv_b, m_gate_bias=m_gate_bias,
             a_q_norm=a_q_norm, a_kv_norm=a_kv_norm, a_w_uq=a_w_uq, a_w_ukv=a_w_ukv, a_q_head_norm=a_q_head_norm,
             a_k_head_norm=a_k_head_norm, c_v_norm=c_v_norm, c_w_s=c_w_s, c_b_s=c_b_s, mix_out_norm=mix_out_norm,
             w_out=w_out, ffn_norm=ffn_norm, w_group=w_group, b_group=b_group, w_expert=w_expert,
             b_expert=b_expert, w1=w1, w3=w3, w2=w2)
    rope = _rope_tables(positions)
    xt = x.reshape(T, D)
    for l in range(w_in.shape[0]):
        xt = _layer(xt, rope, _pack_layer(p, l, tm), B, S, tm)
    return xt.reshape(B, S, D)
```

```python
import functools

import jax
import jax.numpy as jnp
from jax import lax
from jax.experimental import pallas as pl
from jax.experimental.pallas import tpu as pltpu

D_MODEL = 1024
M_HEADS = 4
M_HEAD_DIM = 96
M_WIDTH = M_HEADS * M_HEAD_DIM
M_CHUNK = 128
CONV_WIDTH = 4
A_HEADS = 6
A_NOPE = 64
A_ROPE = 32
A_QK_DIM = A_NOPE + A_ROPE
A_V_DIM = 64
A_WIDTH = A_HEADS * A_V_DIM
A_Q_RANK = 256
A_KV_RANK = 128
ROPE_THETA = 10000.0
C_GROUPS = 4
C_GROUP_DIM = 64
C_WIDTH = C_GROUPS * C_GROUP_DIM
C_CHUNK = 128
N_GROUPS = 4
EXPERTS_PER_GROUP = 8
N_EXPERTS = N_GROUPS * EXPERTS_PER_GROUP
D_EXPERT = 256
EPS = 1e-6

LANE = 128
HEAD_PAD = LANE
M_PAD = M_HEADS * HEAD_PAD
A_PAD = A_HEADS * HEAD_PAD
VMEM_LIMIT = 48 * 1024 * 1024

_COL_QK = (0, 2 * M_PAD)
_COL_V = (_COL_QK[1], _COL_QK[1] + M_PAD)
_COL_O = (_COL_V[1], _COL_V[1] + M_PAD)
_COL_G = (_COL_O[1], _COL_O[1] + LANE)
_COL_CQ = (_COL_G[1], _COL_G[1] + A_Q_RANK)
_COL_CKR = (_COL_CQ[1], _COL_CQ[1] + 2 * LANE)
_COL_CU = (_COL_CKR[1], _COL_CKR[1] + C_WIDTH)
_COL_CV = (_COL_CU[1], _COL_CU[1] + C_WIDTH)
D_IN_PAD = _COL_CV[1]

ATT_BLOCK = 256
NEG = -1e30

_BF = jnp.bfloat16
_F32 = jnp.float32


def _dot(a, b):
    return jnp.dot(a, b, preferred_element_type=_F32)


def _dot_nt(a, b):
    return lax.dot_general(a, b, (((1,), (1,)), ((), ())), preferred_element_type=_F32)


def _dot_tn(a, b):
    return lax.dot_general(a, b, (((0,), (0,)), ((), ())), preferred_element_type=_F32)


def _split2(a):
    hi = a.astype(_BF)
    lo = (a - hi.astype(_F32)).astype(_BF)
    return hi, lo


def _split3(a):
    hi = a.astype(_BF)
    r1 = a - hi.astype(_F32)
    mid = r1.astype(_BF)
    lo = (r1 - mid.astype(_F32)).astype(_BF)
    return hi, mid, lo


def _group_sum(a, ones_bd):
    hi, lo = _split2(a)
    return _dot(hi, ones_bd) + _dot(lo, ones_bd)


def _params(n_axes):
    return pltpu.CompilerParams(dimension_semantics=("arbitrary",) * n_axes, vmem_limit_bytes=VMEM_LIMIT)


def _const_spec(shape):
    nd = len(shape)
    return pl.BlockSpec(shape, lambda *_: (0,) * nd)


def _token_tile(T):
    for tm in (512, 256, 128):
        if T % tm == 0:
            return tm
    raise ValueError(f"token count {T} must be a multiple of 128")


def _moe_group(T, tm):
    for tg in (2048, 1024, 512, 256, 128):
        if T % tg == 0 and tg % tm == 0:
            return tg
    raise ValueError(f"no MoE token group for {T} tokens with router tile {tm}")


def _rope_kernel(pos_ref, invf_ref, sgn_ref, out_ref):
    S = pos_ref.shape[-1]
    pos = pos_ref[0].astype(_F32)
    ang = invf_ref[...] * pos
    c = jnp.cos(ang)
    s = jnp.sin(ang) * sgn_ref[...]
    for j in range(S // LANE):
        out_ref[0, j * LANE:(j + 1) * LANE, 0:LANE] = c[:, j * LANE:(j + 1) * LANE].T
        out_ref[0, j * LANE:(j + 1) * LANE, LANE:2 * LANE] = s[:, j * LANE:(j + 1) * LANE].T


def _rope_tables(positions):
    B, S = positions.shape
    inv_freq = 1.0 / (ROPE_THETA ** (jnp.arange(0, A_ROPE, 2, dtype=_F32) / A_ROPE))
    half = A_ROPE // 2
    invf = jnp.zeros((LANE,), _F32).at[A_NOPE:A_NOPE + A_ROPE].set(jnp.concatenate([inv_freq, inv_freq]))
    sgn = jnp.zeros((LANE,), _F32).at[A_NOPE:A_NOPE + half].set(-1.0).at[A_NOPE + half:A_NOPE + A_ROPE].set(1.0)
    return pl.pallas_call(
        _rope_kernel,
        grid=(B,),
        in_specs=[pl.BlockSpec((1, 1, S), lambda b: (b, 0, 0)), _const_spec((LANE, 1)), _const_spec((LANE, 1))],
        out_specs=pl.BlockSpec((1, S, 2 * LANE), lambda b: (b, 0, 0)),
        out_shape=jax.ShapeDtypeStruct((B, S, 2 * LANE), _F32),
        compiler_params=_params(1),
        name="rope_tables",
    )(positions.reshape(B, 1, S), invf.reshape(LANE, 1), sgn.reshape(LANE, 1)).reshape(B * S, 2 * LANE)


def _inproj_kernel(x_ref, g_ref, w_ref, cvg_ref, ws_ref, bs_ref, cg_ref, ones_ref,
                   qk_ref, v_ref, o_ref, gate_ref, cq_ref, ckr_ref, yc_ref):
    tm = x_ref.shape[0]
    x = x_ref[...]
    h = (x * lax.rsqrt(jnp.mean(x * x, axis=-1, keepdims=True) + EPS) * g_ref[...]).astype(_BF)

    def proj(col):
        return _dot(h, w_ref[:, col[0]:col[1]])

    qk_ref[...] = proj(_COL_QK).astype(_BF)
    v_ref[...] = proj(_COL_V).astype(_BF)
    o_ref[...] = proj(_COL_O).astype(_BF)
    gate_ref[...] = proj(_COL_G)
    cq_ref[...] = proj(_COL_CQ).astype(_BF)
    ckr_ref[...] = proj(_COL_CKR).astype(_BF)

    ones_bd = ones_ref[...]
    u = jax.nn.gelu(proj(_COL_CU))
    v = jax.nn.gelu(proj(_COL_CV))
    vn = v * lax.rsqrt(_group_sum(v * v, ones_bd) * (1.0 / C_GROUP_DIM) + EPS) * cvg_ref[...]
    vnb = vn.astype(_BF)
    lane = lax.broadcasted_iota(jnp.int32, (C_CHUNK, C_WIDTH), 1)
    for c in range(tm // C_CHUNK):
        rows = slice(c * C_CHUNK, (c + 1) * C_CHUNK)
        vc = vnb[rows]
        mixed = _dot(ws_ref[0], vc)
        for g in range(1, C_GROUPS):
            mixed = jnp.where(lane >= g * C_GROUP_DIM, _dot(ws_ref[g], vc), mixed)
        hc = u[rows] * (mixed + bs_ref[...])
        y = hc * lax.rsqrt(_group_sum(hc * hc, ones_bd) * (1.0 / C_GROUP_DIM) + EPS) * cg_ref[...]
        yc_ref[rows, :] = y.astype(_BF)


def _inproj(x, lw, tm):
    T = x.shape[0]

    def tok(width):
        return pl.BlockSpec((tm, width), lambda i: (i, 0))

    outs = [(2 * M_PAD, _BF), (M_PAD, _BF), (M_PAD, _BF), (LANE, _F32), (A_Q_RANK, _BF), (2 * LANE, _BF), (C_WIDTH, _BF)]
    return pl.pallas_call(
        _inproj_kernel,
        grid=(T // tm,),
        in_specs=[tok(D_MODEL), _const_spec((1, D_MODEL)), _const_spec((D_MODEL, D_IN_PAD)),
                  _const_spec((1, C_WIDTH)), _const_spec((C_GROUPS, C_CHUNK, C_CHUNK)),
                  _const_spec((C_CHUNK, C_WIDTH)), _const_spec((1, C_WIDTH)), _const_spec((C_WIDTH, C_WIDTH))],
        out_specs=[tok(w) for w, _ in outs],
        out_shape=[jax.ShapeDtypeStruct((T, w), dt) for w, dt in outs],
        compiler_params=_params(1),
        name="inproj_gating",
    )(x, lw["attn_norm"], lw["w_in"], lw["c_v_norm"], lw["c_w_s"], lw["c_b_s"], lw["c_out_norm"], lw["ones_bd"])


def _mlstm_kernel(qk_ref, v_ref, o_ref, gate_ref, cw_ref, cb_ref, gb_ref, ng_ref, tri_ref,
                  y_ref, ct_ref, n_ref, m_ref):
    S = qk_ref.shape[0]
    L = M_CHUNK
    ct_ref[...] = jnp.zeros_like(ct_ref)
    n_ref[...] = jnp.zeros_like(n_ref)
    m_ref[...] = jnp.zeros_like(m_ref)
    lane = lax.broadcasted_iota(jnp.int32, (L, LANE), 1)
    causal = lax.broadcasted_iota(jnp.int32, (L, L), 0) >= lax.broadcasted_iota(jnp.int32, (L, L), 1)
    k_scale = M_HEAD_DIM ** -0.5
    halo = 16

    def chunk(c, carry):
        r0 = pl.multiple_of(c * L, L)
        cur = qk_ref[pl.ds(r0, L), :].astype(_F32)
        p0 = pl.multiple_of(jnp.maximum(r0 - halo, 0), halo)
        prev = jnp.where(c > 0, qk_ref[pl.ds(p0, halo), :].astype(_F32), 0.0)
        ext = jnp.concatenate([prev, cur], axis=0)
        conv = cb_ref[...] + ext[halo:] * cw_ref[CONV_WIDTH - 1:CONV_WIDTH, :]
        for j in range(CONV_WIDTH - 1):
            shift = CONV_WIDTH - 1 - j
            conv = conv + pltpu.roll(ext, shift, 0)[halo:] * cw_ref[j:j + 1, :]
        qk = conv * jax.nn.sigmoid(conv)

        G = gate_ref[pl.ds(r0, L), :] + gb_ref[...]
        logf = jnp.minimum(G, 0.0) - jnp.log1p(jnp.exp(-jnp.abs(G)))
        Z = jnp.where(lane < M_HEADS, G, jnp.where(lane < 2 * M_HEADS, logf, 0.0))
        tri = tri_ref[...]
        zh, zm, zl = _split3(Z)
        cum = _dot(tri, zh) + _dot(tri, zm) + _dot(tri, zl)
        ZT = Z.T
        cumT = cum.T

        for h in range(M_HEADS):
            hs = slice(h * HEAD_PAD, (h + 1) * HEAD_PAD)
            q = qk[:, h * HEAD_PAD:(h + 1) * HEAD_PAD]
            k = qk[:, M_PAD + h * HEAD_PAD:M_PAD + (h + 1) * HEAD_PAD] * k_scale
            v = v_ref[pl.ds(r0, L), hs]
            qb = q.astype(_BF)
            kb = k.astype(_BF)
            b_col = cum[:, M_HEADS + h:M_HEADS + h + 1]
            i_col = Z[:, h:h + 1]
            b_row = cumT[M_HEADS + h:M_HEADS + h + 1, :]
            i_row = ZT[h:h + 1, :]
            m_prev = m_ref[h][:, 0:1]
            ct = ct_ref[h]
            n = n_ref[h]

            d = jnp.where(causal, b_col - b_row + i_row, NEG)
            inter = b_col + m_prev
            m_row = jnp.maximum(inter, jnp.max(d, axis=-1, keepdims=True))
            w_intra = jnp.exp(d - m_row)
            w_inter = jnp.exp(inter - m_row)
            s = _dot_nt(qb, kb) * w_intra
            num = _dot(s.astype(_BF), v) + w_inter * _dot(qb, ct.astype(_BF))
            den = jnp.sum(s, axis=-1, keepdims=True) + w_inter * jnp.sum(q * n, axis=-1, keepdims=True)
            hval = num / jnp.maximum(jnp.abs(den), jnp.exp(-m_row))

            b_last = b_col[L - 1:L, :]
            g_row = b_last - b_row + i_row
            g_col = b_last - b_col + i_col
            m_new = jnp.maximum(b_last + m_prev, jnp.max(g_row, axis=-1, keepdims=True))
            a = jnp.exp(b_last + m_prev - m_new)
            wg = jnp.exp(g_col - m_new)
            ct_ref[h] = a * ct + _dot_tn(kb, (wg * v.astype(_F32)).astype(_BF))
            n_ref[h] = a * n + jnp.sum(wg * k, axis=0, keepdims=True)
            m_ref[h] = jnp.broadcast_to(m_new, (1, LANE))

            hm = jax.nn.sigmoid(o_ref[pl.ds(r0, L), hs].astype(_F32)) * hval
            ms = jnp.sum(hm * hm, axis=-1, keepdims=True) * (1.0 / M_HEAD_DIM)
            y_ref[pl.ds(r0, L), hs] = (hm * lax.rsqrt(ms + EPS) * ng_ref[:, hs]).astype(_BF)
        return carry

    lax.fori_loop(0, S // L, chunk, 0)


def _mlstm(qk, v, o, gates, lw, B, S):
    T = B * S

    def seq(width):
        return pl.BlockSpec((S, width), lambda b: (b, 0))

    return pl.pallas_call(
        _mlstm_kernel,
        grid=(B,),
        in_specs=[seq(2 * M_PAD), seq(M_PAD), seq(M_PAD), seq(LANE),
                  _const_spec((CONV_WIDTH, 2 * M_PAD)), _const_spec((1, 2 * M_PAD)), _const_spec((1, LANE)),
                  _const_spec((1, M_PAD)), _const_spec((M_CHUNK, M_CHUNK))],
        out_specs=seq(M_PAD),
        out_shape=jax.ShapeDtypeStruct((T, M_PAD), _BF),
        scratch_shapes=[pltpu.VMEM((M_HEADS, HEAD_PAD, HEAD_PAD), _F32),
                        pltpu.VMEM((M_HEADS, 1, LANE), _F32),
                        pltpu.VMEM((M_HEADS, 1, LANE), _F32)],
        compiler_params=_params(1),
        name="mlstm",
    )(qk, v, o, gates, lw["m_conv_w"], lw["m_conv_b"], lw["m_gate_bias"], lw["m_out_norm"], lw["tri_incl"])


def _mla_kernel(cq_ref, ckr_ref, rope_ref, qg_ref, kvg_ref, wuq_ref, wukv_ref, qhg_ref, khg_ref, ag_ref,
                y_ref, k_scr, v_scr, q_scr, m_scr, l_scr, acc_scr):
    S = ckr_ref.shape[0]
    BQ = ATT_BLOCK
    qi = pl.program_id(1)
    lane = lax.broadcasted_iota(jnp.int32, (BQ, LANE), 1)
    swap_lo = lane < A_NOPE + A_ROPE // 2
    inv_qk = 1.0 / A_QK_DIM

    def rope(xn, rt, sg):
        sw = jnp.where(swap_lo, pltpu.roll(xn, LANE - A_ROPE // 2, 1), pltpu.roll(xn, A_ROPE // 2, 1))
        return xn * rt + sw * sg

    @pl.when(qi == 0)
    def _build_kv():
        def kv_block(j, carry):
            r0 = pl.multiple_of(j * BQ, BQ)
            ck = ckr_ref[pl.ds(r0, BQ), :].astype(_F32)
            ckv = ck[:, :LANE]
            k_rope = ck[:, LANE:]
            kvn = (ckv * lax.rsqrt(jnp.mean(ckv * ckv, axis=-1, keepdims=True) + EPS) * kvg_ref[...]).astype(_BF)
            kv = _dot(kvn, wukv_ref[...])
            v_scr[pl.ds(r0, BQ), :] = kv[:, A_PAD:].astype(_BF)
            kr = rope(k_rope * khg_ref[...], rope_ref[pl.ds(r0, BQ), 0:LANE], rope_ref[pl.ds(r0, BQ), LANE:2 * LANE])
            ss_rope = jnp.sum(k_rope * k_rope, axis=-1, keepdims=True)
            for h in range(A_HEADS):
                hs = slice(h * HEAD_PAD, (h + 1) * HEAD_PAD)
                kh = kv[:, hs]
                r = lax.rsqrt((jnp.sum(kh * kh, axis=-1, keepdims=True) + ss_rope) * inv_qk + EPS)
                k_scr[pl.ds(r0, BQ), hs] = ((kh * khg_ref[...] + kr) * r).astype(_BF)
            return carry

        lax.fori_loop(0, S // BQ, kv_block, 0)

    q0 = pl.multiple_of(qi * BQ, BQ)
    cq = cq_ref[...].astype(_F32)
    qn = (cq * lax.rsqrt(jnp.mean(cq * cq, axis=-1, keepdims=True) + EPS) * qg_ref[...]).astype(_BF)
    qall = _dot(qn, wuq_ref[...])
    rt = rope_ref[pl.ds(q0, BQ), 0:LANE]
    sg = rope_ref[pl.ds(q0, BQ), LANE:2 * LANE]
    scale = (A_QK_DIM ** -0.5) * 1.4426950408889634
    for h in range(A_HEADS):
        hs = slice(h * HEAD_PAD, (h + 1) * HEAD_PAD)
        qh = qall[:, hs]
        r = lax.rsqrt(jnp.sum(qh * qh, axis=-1, keepdims=True) * inv_qk + EPS)
        q_scr[h] = (rope(qh * r * qhg_ref[...], rt, sg) * scale).astype(_BF)
    m_scr[...] = jnp.full(m_scr.shape, NEG, _F32)
    l_scr[...] = jnp.zeros(l_scr.shape, _F32)
    acc_scr[...] = jnp.zeros(acc_scr.shape, _F32)
    col_minus_row = lax.broadcasted_iota(jnp.int32, (BQ, BQ), 1) - lax.broadcasted_iota(jnp.int32, (BQ, BQ), 0)

    def kblock(j, masked):
        k0 = pl.multiple_of(j * BQ, BQ)
        for h in range(A_HEADS):
            hs = slice(h * HEAD_PAD, (h + 1) * HEAD_PAD)
            ps = slice((h // 2) * LANE, (h // 2 + 1) * LANE)
            s = _dot_nt(q_scr[h], k_scr[pl.ds(k0, BQ), hs])
            if masked:
                s = jnp.where(col_minus_row <= 0, s, NEG)
            m_prev = m_scr[h]
            m_new = jnp.maximum(m_prev, jnp.max(s, axis=-1, keepdims=True))
            alpha = jnp.exp2(m_prev - m_new)
            pe = jnp.exp2(s - jnp.concatenate([m_new] * (BQ // LANE), axis=1))
            l_scr[h] = alpha * l_scr[h] + jnp.sum(pe, axis=-1, keepdims=True)
            acc_scr[h] = alpha * acc_scr[h] + _dot(pe.astype(_BF), v_scr[pl.ds(k0, BQ), ps])
            m_scr[h] = m_new

    def full_block(j, carry):
        kblock(j, False)
        return carry

    lax.fori_loop(0, qi, full_block, 0)
    kblock(qi, True)

    for p in range(A_HEADS // 2):
        ps = slice(p * LANE, (p + 1) * LANE)
        outs = [acc_scr[2 * p + hh] / l_scr[2 * p + hh] for hh in range(2)]
        lo = lane < A_V_DIM
        o = jnp.where(lo, outs[0], outs[1])
        o2 = o * o
        ms_lo = jnp.sum(jnp.where(lo, o2, 0.0), axis=-1, keepdims=True) * (1.0 / A_V_DIM)
        ms_hi = jnp.sum(jnp.where(lo, 0.0, o2), axis=-1, keepdims=True) * (1.0 / A_V_DIM)
        rs = jnp.where(lo, lax.rsqrt(ms_lo + EPS), lax.rsqrt(ms_hi + EPS))
        y_ref[:, ps] = (o * rs * ag_ref[:, ps]).astype(_BF)


def _mla(cq, ckr, rope, lw, B, S):
    T = B * S
    BQ = ATT_BLOCK
    nq = S // BQ
    return pl.pallas_call(
        _mla_kernel,
        grid=(B, nq),
        in_specs=[pl.BlockSpec((BQ, A_Q_RANK), lambda b, q: (b * nq + q, 0)),
                  pl.BlockSpec((S, 2 * LANE), lambda b, q: (b, 0)),
                  pl.BlockSpec((S, 2 * LANE), lambda b, q: (b, 0)),
                  _const_spec((1, A_Q_RANK)), _const_spec((1, A_KV_RANK)),
                  _const_spec((A_Q_RANK, A_PAD)), _const_spec((A_KV_RANK, A_PAD + A_WIDTH)),
                  _const_spec((1, LANE)), _const_spec((1, LANE)), _const_spec((1, A_WIDTH))],
        out_specs=pl.BlockSpec((BQ, A_WIDTH), lambda b, q: (b * nq + q, 0)),
        out_shape=jax.ShapeDtypeStruct((T, A_WIDTH), _BF),
        scratch_shapes=[pltpu.VMEM((S, A_PAD), _BF), pltpu.VMEM((S, A_WIDTH), _BF),
                        pltpu.VMEM((A_HEADS, BQ, HEAD_PAD), _BF)] + [pltpu.VMEM((A_HEADS, BQ, LANE), _F32)] * 3,
        compiler_params=_params(2),
        name="mla",
    )(cq, ckr, rope, lw["a_q_norm"], lw["a_kv_norm"], lw["a_w_uq"], lw["a_w_ukv"],
      lw["a_q_head_norm"], lw["a_k_head_norm"], lw["a_out_norm"])


def _outproj_router_kernel(ym_ref, ya_ref, yc_ref, x_ref, wm_ref, wa_ref, wc_ref, g_ref, wr_hi_ref, wr_lo_ref,
                           br_ref, tri_ref, xmid_ref, ri_ref, rw_ref, cnt_ref, *, tiles_per_group):
    tm = x_ref.shape[0]
    i = pl.program_id(0)

    @pl.when(i % tiles_per_group == 0)
    def _():
        cnt_ref[...] = jnp.zeros_like(cnt_ref)

    y = _dot(ym_ref[...], wm_ref[...]) + _dot(ya_ref[...], wa_ref[...]) + _dot(yc_ref[...], wc_ref[...])
    xm = x_ref[...] + y
    xmid_ref[...] = xm
    xn = xm * lax.rsqrt(jnp.mean(xm * xm, axis=-1, keepdims=True) + EPS) * g_ref[...]

    xh, xl = _split2(xn)
    logits = _dot(xh, wr_hi_ref[...]) + _dot(xh, wr_lo_ref[...]) + _dot(xl, wr_hi_ref[...])
    biased = logits + br_ref[...]
    lane = lax.broadcasted_iota(jnp.int32, (tm, LANE), 1)
    lane_f = lane.astype(_F32)

    def lane_max(a):
        return jnp.max(a, axis=-1, keepdims=True)

    def lane_sum(a):
        return jnp.sum(a, axis=-1, keepdims=True)

    def first_argmax(a, amax):
        return jnp.min(jnp.where(a == amax, lane_f, float(LANE)), axis=-1, keepdims=True).astype(jnp.int32)

    gmask = lane < N_GROUPS
    gl = jnp.where(gmask, logits, NEG)
    gexp = jnp.where(gmask, jnp.exp(gl - lane_max(gl)), 0.0)
    gsc = jnp.where(gmask, biased, NEG)
    g_sel = first_argmax(gsc, lane_max(gsc))
    g_gate = lane_sum(jnp.where(lane == g_sel, gexp, 0.0)) / lane_sum(gexp)

    e_lo = N_GROUPS + EXPERTS_PER_GROUP * g_sel
    emask = jnp.logical_and(lane >= e_lo, lane < e_lo + EXPERTS_PER_GROUP)
    el = jnp.where(emask, logits, NEG)
    eexp = jnp.where(emask, jnp.exp(el - lane_max(el)), 0.0)
    esc = jnp.where(emask, biased, NEG)
    i1 = first_argmax(esc, lane_max(esc))
    esc2 = jnp.where(lane == i1, NEG, esc)
    i2 = first_argmax(esc2, lane_max(esc2))
    oh1 = lane == i1
    oh2 = lane == i2
    p1 = lane_sum(jnp.where(oh1, eexp, 0.0))
    p2 = lane_sum(jnp.where(oh2, eexp, 0.0))
    w1 = g_gate * p1 / (p1 + p2)
    w2 = g_gate * p2 / (p1 + p2)

    cnt = jnp.where(oh1, 1.0, 0.0) + jnp.where(oh2, 1.0, 0.0)
    base = cnt_ref[0] + _dot(tri_ref[...], cnt.astype(_BF))
    rank1 = lane_sum(jnp.where(oh1, base, 0.0)).astype(jnp.int32)
    rank2 = lane_sum(jnp.where(oh2, base, 0.0)).astype(jnp.int32)
    cnt_ref[0] = cnt_ref[0] + jnp.sum(cnt, axis=0, keepdims=True)

    ri = jnp.where(lane == 0, i1 - N_GROUPS, jnp.where(lane == 1, i2 - N_GROUPS,
                                                       jnp.where(lane == 2, rank1, jnp.where(lane == 3, rank2, 0))))
    ri_ref[...] = ri
    rw_ref[...] = jnp.where(lane == 0, w1, jnp.where(lane == 1, w2, 0.0))


def _outproj_router(ym, ya, yc, x, lw, tm, tg):
    T = x.shape[0]
    tiles_per_group = tg // tm

    def tok(width):
        return pl.BlockSpec((tm, width), lambda i: (i, 0))

    return pl.pallas_call(
        functools.partial(_outproj_router_kernel, tiles_per_group=tiles_per_group),
        grid=(T // tm,),
        in_specs=[tok(M_PAD), tok(A_WIDTH), tok(C_WIDTH), tok(D_MODEL),
                  _const_spec((M_PAD, D_MODEL)), _const_spec((A_WIDTH, D_MODEL)), _const_spec((C_WIDTH, D_MODEL)),
                  _const_spec((1, D_MODEL)), _const_spec((D_MODEL, LANE)), _const_spec((D_MODEL, LANE)),
                  _const_spec((1, LANE)), _const_spec((tm, tm))],
        out_specs=[tok(D_MODEL), tok(LANE), tok(LANE),
                   pl.BlockSpec((1, 1, LANE), lambda i: (i // tiles_per_group, 0, 0))],
        out_shape=[jax.ShapeDtypeStruct((T, D_MODEL), _F32),
                   jax.ShapeDtypeStruct((T, LANE), jnp.int32), jax.ShapeDtypeStruct((T, LANE), _F32),
                   jax.ShapeDtypeStruct((T // tg, 1, LANE), _F32)],
        compiler_params=_params(1),
        name="outproj_router",
    )(ym, ya, yc, x, lw["w_out_m"], lw["w_out_a"], lw["w_out_c"], lw["ffn_norm"], lw["w_router_hi"],
      lw["w_router_lo"], lw["b_router"], lw["tri_strict"])


SLAB = 8
MOE_CHUNK = 256
MOE_ROWS = 8
CONV_ROWS = 256
INVERT_UNROLL = 8


def _moe_kernel(off_ref, e_ref, rk_ref, w_ref, xmid_ref, g_ref, w1_ref, w3_ref, w2_ref, out_ref,
                xs_ref, acc_ref, g_buf, y_buf, inv_ref, wl_ref):
    i = pl.program_id(0)
    e = pl.program_id(1)
    tg = xmid_ref.shape[0]
    CH = MOE_CHUNK

    def slab_rows(r0, j, n):
        return pl.ds(pl.multiple_of(r0 * SLAB, SLAB) + j, n, stride=SLAB)

    def slab(row):
        return pl.ds(pl.multiple_of(row * SLAB, SLAB), SLAB)

    @pl.when(jnp.logical_and(i == 0, e == 0))
    def _():
        g_buf[...] = jnp.zeros_like(g_buf)

    @pl.when(e == 0)
    def _prepare_group():
        def convert(rb, carry):
            r0 = pl.multiple_of(rb * CONV_ROWS, CONV_ROWS)
            xm = xmid_ref[pl.ds(r0, CONV_ROWS), :]
            xn = xm * lax.rsqrt(jnp.mean(xm * xm, axis=-1, keepdims=True) + EPS) * g_ref[...]
            for j in range(SLAB):
                xs_ref[slab_rows(r0, j, CONV_ROWS), :] = xn[:, j * LANE:(j + 1) * LANE]
                acc_ref[slab_rows(r0, j, CONV_ROWS), :] = xm[:, j * LANE:(j + 1) * LANE]
            return carry

        lax.fori_loop(0, tg // CONV_ROWS, convert, 0)
        acc_ref[slab(tg), :] = jnp.zeros((SLAB, LANE), _F32)

        def invert(it, carry):
            a0 = it * INVERT_UNROLL
            slots = []
            for k in range(INVERT_UNROLL):
                a = a0 + k
                pos = off_ref[i * (N_EXPERTS + 1) + e_ref[0, 0, a]] + rk_ref[0, 0, a]
                slots.append((pos, it * (INVERT_UNROLL // 2) + k // 2, w_ref[0, 0, a]))
            for pos, tok, wgt in slots:
                inv_ref[pos] = tok
                wl_ref[pos] = wgt
            return carry

        lax.fori_loop(0, 2 * tg // INVERT_UNROLL, invert, 0)
        for r in range(MOE_ROWS):
            inv_ref[2 * tg + r] = 0
            wl_ref[2 * tg + r] = 0.0

    base = off_ref[i * (N_EXPERTS + 1) + e]
    end = off_ref[i * (N_EXPERTS + 1) + e + 1]

    def chunk(c, carry):
        b0 = base + c * CH
        n_steps = (jnp.minimum(end - b0, CH) + MOE_ROWS - 1) // MOE_ROWS

        def gather(it, cc):
            p0 = b0 + it * MOE_ROWS
            d0 = pl.multiple_of(it * (MOE_ROWS * SLAB), MOE_ROWS * SLAB)
            for r in range(MOE_ROWS):
                g_buf[pl.ds(d0 + r * SLAB, SLAB), :] = xs_ref[slab(inv_ref[p0 + r]), :]
            return cc

        lax.fori_loop(0, n_steps, gather, 0)
        x = jnp.concatenate([g_buf[slab_rows(0, j, CH), :] for j in range(SLAB)], axis=1).astype(_BF)
        hid = (jax.nn.silu(_dot(x, w1_ref[0])) * _dot(x, w3_ref[0])).astype(_BF)
        y = _dot(hid, w2_ref[0])
        for j in range(SLAB):
            y_buf[slab_rows(0, j, CH), :] = y[:, j * LANE:(j + 1) * LANE]

        def scatter(it, cc):
            p0 = b0 + it * MOE_ROWS
            d0 = pl.multiple_of(it * (MOE_ROWS * SLAB), MOE_ROWS * SLAB)
            updates = []
            for r in range(MOE_ROWS):
                ok = p0 + r < end
                tok = jnp.where(ok, inv_ref[p0 + r], tg)
                wgt = jnp.where(ok, wl_ref[p0 + r], 0.0)
                updates.append((tok, acc_ref[slab(tok), :] + wgt * y_buf[pl.ds(d0 + r * SLAB, SLAB), :]))
            for tok, val in updates:
                acc_ref[slab(tok), :] = val
            return cc

        lax.fori_loop(0, n_steps, scatter, 0)
        return carry

    lax.fori_loop(0, (end - base + CH - 1) // CH, chunk, 0)

    @pl.when(e == N_EXPERTS - 1)
    def _finish_group():
        def convert(rb, carry):
            r0 = pl.multiple_of(rb * CONV_ROWS, CONV_ROWS)
            for j in range(SLAB):
                out_ref[pl.ds(r0, CONV_ROWS), j * LANE:(j + 1) * LANE] = acc_ref[slab_rows(r0, j, CONV_ROWS), :]
            return carry

        lax.fori_loop(0, tg // CONV_ROWS, convert, 0)


def _moe(xmid, ri, rw, counts, lw, tg):
    T = xmid.shape[0]
    ng = T // tg
    cnt = counts[:, 0, N_GROUPS:N_GROUPS + N_EXPERTS].astype(jnp.int32)
    ends = jnp.cumsum(cnt, axis=1)
    offs = jnp.concatenate([jnp.zeros((ng, 1), jnp.int32), ends], axis=1).reshape(ng * (N_EXPERTS + 1))
    per_group = lambda a: a.reshape(ng, 1, 2 * tg)
    smem = pl.BlockSpec((1, 1, 2 * tg), lambda i, e, off: (i, 0, 0), memory_space=pltpu.SMEM)
    grid_spec = pltpu.PrefetchScalarGridSpec(
        num_scalar_prefetch=1,
        grid=(ng, N_EXPERTS),
        in_specs=[smem, smem, smem,
                  pl.BlockSpec((tg, D_MODEL), lambda i, e, off: (i, 0), pipeline_mode=pl.Buffered(1)),
                  pl.BlockSpec((1, D_MODEL), lambda i, e, off: (0, 0)),
                  pl.BlockSpec((1, D_MODEL, D_EXPERT), lambda i, e, off: (e, 0, 0)),
                  pl.BlockSpec((1, D_MODEL, D_EXPERT), lambda i, e, off: (e, 0, 0)),
                  pl.BlockSpec((1, D_EXPERT, D_MODEL), lambda i, e, off: (e, 0, 0))],
        out_specs=pl.BlockSpec((tg, D_MODEL), lambda i, e, off: (i, 0)),
        scratch_shapes=[pltpu.VMEM((tg * SLAB, LANE), _F32), pltpu.VMEM(((tg + 1) * SLAB, LANE), _F32),
                        pltpu.VMEM((MOE_CHUNK * SLAB, LANE), _F32), pltpu.VMEM((MOE_CHUNK * SLAB, LANE), _F32),
                        pltpu.SMEM((2 * tg + MOE_ROWS,), jnp.int32), pltpu.SMEM((2 * tg + MOE_ROWS,), _F32)],
    )
    return pl.pallas_call(
        _moe_kernel,
        grid_spec=grid_spec,
        out_shape=jax.ShapeDtypeStruct((T, D_MODEL), _F32),
        compiler_params=_params(2),
        name="moe",
    )(offs, per_group(ri[:, 0:2]), per_group(ri[:, 2:4]), per_group(rw[:, 0:2]), xmid, lw["ffn_norm"],
      lw["w1"], lw["w3"], lw["w2"])


def _pad_heads(w, n_heads, dim):
    lead = w.shape[:-1]
    w = w.reshape(lead + (n_heads, dim))
    w = jnp.pad(w, [(0, 0)] * len(lead) + [(0, 0), (0, HEAD_PAD - dim)])
    return w.reshape(lead + (n_heads * HEAD_PAD,))


def _pack_layer(p, l, tm):
    w_in = p["w_in"][l]
    sizes = (M_WIDTH, M_WIDTH, M_WIDTH, M_WIDTH, M_HEADS, M_HEADS, A_Q_RANK, A_KV_RANK, A_ROPE, C_WIDTH, C_WIDTH)
    cols, off = [], 0
    for s in sizes:
        cols.append(w_in[:, off:off + s])
        off += s
    mq, mk, mv, mo, mi, mf, cq, ckv, krope, cu, cv = cols
    zeros = lambda n: jnp.zeros((D_MODEL, n), _F32)
    w_all = jnp.concatenate([
        _pad_heads(mq, M_HEADS, M_HEAD_DIM), _pad_heads(mk, M_HEADS, M_HEAD_DIM),
        _pad_heads(mv, M_HEADS, M_HEAD_DIM), _pad_heads(mo, M_HEADS, M_HEAD_DIM),
        mi, mf, zeros(LANE - 2 * M_HEADS),
        cq, ckv, zeros(A_NOPE), krope, zeros(LANE - A_NOPE - A_ROPE), cu, cv], axis=1).astype(_BF)

    g = p["mix_out_norm"][l]
    w_out = p["w_out"][l]
    w_out_m = jnp.pad(w_out[:M_WIDTH].reshape(M_HEADS, M_HEAD_DIM, D_MODEL),
                      ((0, 0), (0, HEAD_PAD - M_HEAD_DIM), (0, 0))).reshape(M_PAD, D_MODEL)

    w_uq = _pad_heads(p["a_w_uq"][l], A_HEADS, A_QK_DIM)
    w_ukv = p["a_w_ukv"][l].reshape(A_KV_RANK, A_HEADS, A_NOPE + A_V_DIM)
    w_uk = _pad_heads(w_ukv[:, :, :A_NOPE].reshape(A_KV_RANK, A_HEADS * A_NOPE), A_HEADS, A_NOPE)
    w_uv = w_ukv[:, :, A_NOPE:].reshape(A_KV_RANK, A_WIDTH)

    w_router = jnp.concatenate([p["w_group"][l], p["w_expert"][l],
                                jnp.zeros((D_MODEL, LANE - N_GROUPS - N_EXPERTS), _F32)], axis=1)
    wr_hi = w_router.astype(_BF)
    wr_lo = (w_router - wr_hi.astype(_F32)).astype(_BF)
    b_router = jnp.concatenate([p["b_group"][l], p["b_expert"][l],
                                jnp.zeros((LANE - N_GROUPS - N_EXPERTS,), _F32)]).reshape(1, LANE)

    tril = jnp.tril(jnp.ones((C_CHUNK, C_CHUNK), _F32))
    lane_pad = lambda a: jnp.pad(a, (0, HEAD_PAD - a.shape[0])).reshape(1, HEAD_PAD)
    return {
        "attn_norm": p["attn_norm"][l].reshape(1, D_MODEL),
        "w_in": w_all,
        "c_v_norm": p["c_v_norm"][l].reshape(1, C_WIDTH),
        "c_w_s": (p["c_w_s"][l] * tril).astype(_BF),
        "c_b_s": jnp.repeat(p["c_b_s"][l].T, C_GROUP_DIM, axis=1),
        "c_out_norm": g[M_WIDTH + A_WIDTH:].reshape(1, C_WIDTH),
        "ones_bd": jnp.kron(jnp.eye(C_GROUPS, dtype=_F32), jnp.ones((C_GROUP_DIM, C_GROUP_DIM), _F32)).astype(_BF),
        "m_conv_w": jnp.concatenate([_pad_heads(p["m_conv_w"][l][:, :M_WIDTH], M_HEADS, M_HEAD_DIM),
                                     _pad_heads(p["m_conv_w"][l][:, M_WIDTH:], M_HEADS, M_HEAD_DIM)], axis=1),
        "m_conv_b": jnp.concatenate([_pad_heads(p["m_conv_b"][l][:M_WIDTH], M_HEADS, M_HEAD_DIM),
                                     _pad_heads(p["m_conv_b"][l][M_WIDTH:], M_HEADS, M_HEAD_DIM)]).reshape(1, 2 * M_PAD),
        "m_gate_bias": jnp.pad(p["m_gate_bias"][l], (0, LANE - 2 * M_HEADS)).reshape(1, LANE),
        "m_out_norm": _pad_heads(g[:M_WIDTH], M_HEADS, M_HEAD_DIM).reshape(1, M_PAD),
        "tri_incl": tril.astype(_BF),
        "a_q_norm": p["a_q_norm"][l].reshape(1, A_Q_RANK),
        "a_kv_norm": p["a_kv_norm"][l].reshape(1, A_KV_RANK),
        "a_w_uq": w_uq.astype(_BF),
        "a_w_ukv": jnp.concatenate([w_uk, w_uv], axis=1).astype(_BF),
        "a_q_head_norm": lane_pad(p["a_q_head_norm"][l]),
        "a_k_head_norm": lane_pad(p["a_k_head_norm"][l]),
        "a_out_norm": g[M_WIDTH:M_WIDTH + A_WIDTH].reshape(1, A_WIDTH),
        "w_out_m": w_out_m.astype(_BF),
        "w_out_a": w_out[M_WIDTH:M_WIDTH + A_WIDTH].astype(_BF),
        "w_out_c": w_out[M_WIDTH + A_WIDTH:].astype(_BF),
        "ffn_norm": p["ffn_norm"][l].reshape(1, D_MODEL),
        "w_router_hi": wr_hi,
        "w_router_lo": wr_lo,
        "b_router": b_router,
        "tri_strict": jnp.tril(jnp.ones((tm, tm), _F32), -1).astype(_BF),
        "w1": p["w1"][l].astype(_BF),
        "w3": p["w3"][l].astype(_BF),
        "w2": p["w2"][l].astype(_BF),
    }


def _layer(x, rope, lw, B, S, tm, tg):
    qk, v, o, gates, cq, ckr, yc = _inproj(x, lw, tm)
    ym = _mlstm(qk, v, o, gates, lw, B, S)
    ya = _mla(cq, ckr, rope, lw, B, S)
    xmid, ri, rw, counts = _outproj_router(ym, ya, yc, x, lw, tm, tg)
    return _moe(xmid, ri, rw, counts, lw, tg)


def kernel(x, positions, attn_norm, w_in, m_conv_w, m_conv_b, m_gate_bias, a_q_norm, a_kv_norm, a_w_uq, a_w_ukv, a_q_head_norm, a_k_head_norm, c_v_norm, c_w_s, c_b_s, mix_out_norm, w_out, ffn_norm, w_group, b_group, w_expert, b_expert, w1, w3, w2):
    B, S, D = x.shape
    assert D == D_MODEL and S % ATT_BLOCK == 0
    T = B * S
    tm = _token_tile(T)
    tg = _moe_group(T, tm)
    p = dict(attn_norm=attn_norm, w_in=w_in, m_conv_w=m_conv_w, m_conv_b=m_conv_b, m_gate_bias=m_gate_bias,
             a_q_norm=a_q_norm, a_kv_norm=a_kv_norm, a_w_uq=a_w_uq, a_w_ukv=a_w_ukv, a_q_head_norm=a_q_head_norm,
             a_k_head_norm=a_k_head_norm, c_v_norm=c_v_norm, c_w_s=c_w_s, c_b_s=c_b_s, mix_out_norm=mix_out_norm,
             w_out=w_out, ffn_norm=ffn_norm, w_group=w_group, b_group=b_group, w_expert=w_expert,
             b_expert=b_expert, w1=w1, w3=w3, w2=w2)
    rope = _rope_tables(positions)
    xt = x.reshape(T, D)
    for l in range(w_in.shape[0]):
        xt = _layer(xt, rope, _pack_layer(p, l, tm), B, S, tm, tg)
    return xt.reshape(B, S, D)
```

```python
import functools

import jax
import jax.numpy as jnp
from jax import lax
from jax.experimental import pallas as pl
from jax.experimental.pallas import tpu as pltpu

D_MODEL = 1024
M_HEADS = 4
M_HEAD_DIM = 96
M_WIDTH = M_HEADS * M_HEAD_DIM
M_CHUNK = 128
CONV_WIDTH = 4
A_HEADS = 6
A_NOPE = 64
A_ROPE = 32
A_QK_DIM = A_NOPE + A_ROPE
A_V_DIM = 64
A_WIDTH = A_HEADS * A_V_DIM
A_Q_RANK = 256
A_KV_RANK = 128
ROPE_THETA = 10000.0
C_GROUPS = 4
C_GROUP_DIM = 64
C_WIDTH = C_GROUPS * C_GROUP_DIM
C_CHUNK = 128
N_GROUPS = 4
EXPERTS_PER_GROUP = 8
N_EXPERTS = N_GROUPS * EXPERTS_PER_GROUP
D_EXPERT = 256
EPS = 1e-6

LANE = 128
HEAD_PAD = LANE
M_PAD = M_HEADS * HEAD_PAD
A_PAD = A_HEADS * HEAD_PAD
VMEM_LIMIT = 48 * 1024 * 1024

_COL_QK = (0, 2 * M_PAD)
_COL_V = (_COL_QK[1], _COL_QK[1] + M_PAD)
_COL_O = (_COL_V[1], _COL_V[1] + M_PAD)
_COL_G = (_COL_O[1], _COL_O[1] + LANE)
_COL_CQ = (_COL_G[1], _COL_G[1] + A_Q_RANK)
_COL_CKR = (_COL_CQ[1], _COL_CQ[1] + 2 * LANE)
_COL_CU = (_COL_CKR[1], _COL_CKR[1] + C_WIDTH)
_COL_CV = (_COL_CU[1], _COL_CU[1] + C_WIDTH)
D_IN_PAD = _COL_CV[1]

ATT_BLOCK = 256
RANK_BITS = 12
RANK_SPAN = 1 << RANK_BITS
NEG = -1e30

_BF = jnp.bfloat16
_F32 = jnp.float32


def _dot(a, b):
    return jnp.dot(a, b, preferred_element_type=_F32)


def _dot_nt(a, b):
    return lax.dot_general(a, b, (((1,), (1,)), ((), ())), preferred_element_type=_F32)


def _dot_tn(a, b):
    return lax.dot_general(a, b, (((0,), (0,)), ((), ())), preferred_element_type=_F32)


def _split2(a):
    hi = a.astype(_BF)
    lo = (a - hi.astype(_F32)).astype(_BF)
    return hi, lo


def _split3(a):
    hi = a.astype(_BF)
    r1 = a - hi.astype(_F32)
    mid = r1.astype(_BF)
    lo = (r1 - mid.astype(_F32)).astype(_BF)
    return hi, mid, lo


def _group_sum(a, ones_bd):
    hi, lo = _split2(a)
    return _dot(hi, ones_bd) + _dot(lo, ones_bd)


def _params(n_axes, flags=None):
    return pltpu.CompilerParams(dimension_semantics=("arbitrary",) * n_axes, vmem_limit_bytes=VMEM_LIMIT,
                                flags=flags)


def _const_spec(shape):
    nd = len(shape)
    return pl.BlockSpec(shape, lambda *_: (0,) * nd)


def _token_tile(T):
    for tm in (512, 256, 128):
        if T % tm == 0:
            return tm
    raise ValueError(f"token count {T} must be a multiple of 128")


def _moe_group(T, tm):
    for tg in (2048, 1024, 512, 256, 128):
        if T % tg == 0 and tg % tm == 0:
            return tg
    raise ValueError(f"no MoE token group for {T} tokens with router tile {tm}")


def _rope_kernel(pos_ref, invf_ref, sgn_ref, out_ref):
    S = pos_ref.shape[-1]
    pos = pos_ref[0].astype(_F32)
    ang = invf_ref[...] * pos
    c = jnp.cos(ang)
    s = jnp.sin(ang) * sgn_ref[...]
    for j in range(S // LANE):
        out_ref[0, j * LANE:(j + 1) * LANE, 0:LANE] = c[:, j * LANE:(j + 1) * LANE].T
        out_ref[0, j * LANE:(j + 1) * LANE, LANE:2 * LANE] = s[:, j * LANE:(j + 1) * LANE].T


def _rope_tables(positions):
    B, S = positions.shape
    inv_freq = 1.0 / (ROPE_THETA ** (jnp.arange(0, A_ROPE, 2, dtype=_F32) / A_ROPE))
    half = A_ROPE // 2
    invf = jnp.zeros((LANE,), _F32).at[A_NOPE:A_NOPE + A_ROPE].set(jnp.concatenate([inv_freq, inv_freq]))
    sgn = jnp.zeros((LANE,), _F32).at[A_NOPE:A_NOPE + half].set(-1.0).at[A_NOPE + half:A_NOPE + A_ROPE].set(1.0)
    return pl.pallas_call(
        _rope_kernel,
        grid=(B,),
        in_specs=[pl.BlockSpec((1, 1, S), lambda b: (b, 0, 0)), _const_spec((LANE, 1)), _const_spec((LANE, 1))],
        out_specs=pl.BlockSpec((1, S, 2 * LANE), lambda b: (b, 0, 0)),
        out_shape=jax.ShapeDtypeStruct((B, S, 2 * LANE), _F32),
        compiler_params=_params(1),
        name="rope_tables",
    )(positions.reshape(B, 1, S), invf.reshape(LANE, 1), sgn.reshape(LANE, 1)).reshape(B * S, 2 * LANE)


def _inproj_kernel(x_ref, g_ref, w_ref, cvg_ref, ws_ref, bs_ref, cg_ref, ones_ref,
                   qk_ref, v_ref, o_ref, gate_ref, cq_ref, ckr_ref, yc_ref):
    tm = x_ref.shape[0]
    x = x_ref[...]
    h = (x * lax.rsqrt(jnp.mean(x * x, axis=-1, keepdims=True) + EPS) * g_ref[...]).astype(_BF)

    def proj(col):
        return _dot(h, w_ref[:, col[0]:col[1]])

    qk_ref[...] = proj(_COL_QK).astype(_BF)
    v_ref[...] = proj(_COL_V).astype(_BF)
    o_ref[...] = proj(_COL_O).astype(_BF)
    gate_ref[...] = proj(_COL_G)
    cq_ref[...] = proj(_COL_CQ).astype(_BF)
    ckr_ref[...] = proj(_COL_CKR).astype(_BF)

    ones_bd = ones_ref[...]
    u = jax.nn.gelu(proj(_COL_CU))
    v = jax.nn.gelu(proj(_COL_CV))
    vn = v * lax.rsqrt(_group_sum(v * v, ones_bd) * (1.0 / C_GROUP_DIM) + EPS) * cvg_ref[...]
    vnb = vn.astype(_BF)
    lane = lax.broadcasted_iota(jnp.int32, (C_CHUNK, C_WIDTH), 1)
    for c in range(tm // C_CHUNK):
        rows = slice(c * C_CHUNK, (c + 1) * C_CHUNK)
        vc = vnb[rows]
        mixed = _dot(ws_ref[0], vc)
        for g in range(1, C_GROUPS):
            mixed = jnp.where(lane >= g * C_GROUP_DIM, _dot(ws_ref[g], vc), mixed)
        hc = u[rows] * (mixed + bs_ref[...])
        y = hc * lax.rsqrt(_group_sum(hc * hc, ones_bd) * (1.0 / C_GROUP_DIM) + EPS) * cg_ref[...]
        yc_ref[rows, :] = y.astype(_BF)


def _inproj(x, lw, tm):
    T = x.shape[0]

    def tok(width):
        return pl.BlockSpec((tm, width), lambda i: (i, 0))

    outs = [(2 * M_PAD, _BF), (M_PAD, _BF), (M_PAD, _BF), (LANE, _F32), (A_Q_RANK, _BF), (2 * LANE, _BF), (C_WIDTH, _BF)]
    return pl.pallas_call(
        _inproj_kernel,
        grid=(T // tm,),
        in_specs=[tok(D_MODEL), _const_spec((1, D_MODEL)), _const_spec((D_MODEL, D_IN_PAD)),
                  _const_spec((1, C_WIDTH)), _const_spec((C_GROUPS, C_CHUNK, C_CHUNK)),
                  _const_spec((C_CHUNK, C_WIDTH)), _const_spec((1, C_WIDTH)), _const_spec((C_WIDTH, C_WIDTH))],
        out_specs=[tok(w) for w, _ in outs],
        out_shape=[jax.ShapeDtypeStruct((T, w), dt) for w, dt in outs],
        compiler_params=_params(1),
        name="inproj_gating",
    )(x, lw["attn_norm"], lw["w_in"], lw["c_v_norm"], lw["c_w_s"], lw["c_b_s"], lw["c_out_norm"], lw["ones_bd"])


def _mlstm_kernel(qk_ref, v_ref, o_ref, gate_ref, cw_ref, cb_ref, gb_ref, ng_ref, tri_ref,
                  y_ref, ct_ref, n_ref, m_ref):
    S = qk_ref.shape[0]
    L = M_CHUNK
    ct_ref[...] = jnp.zeros_like(ct_ref)
    n_ref[...] = jnp.zeros_like(n_ref)
    m_ref[...] = jnp.zeros_like(m_ref)
    lane = lax.broadcasted_iota(jnp.int32, (L, LANE), 1)
    causal = lax.broadcasted_iota(jnp.int32, (L, L), 0) >= lax.broadcasted_iota(jnp.int32, (L, L), 1)
    k_scale = M_HEAD_DIM ** -0.5
    halo = 16

    def chunk(c, carry):
        r0 = pl.multiple_of(c * L, L)
        cur = qk_ref[pl.ds(r0, L), :].astype(_F32)
        p0 = pl.multiple_of(jnp.maximum(r0 - halo, 0), halo)
        prev = jnp.where(c > 0, qk_ref[pl.ds(p0, halo), :].astype(_F32), 0.0)
        ext = jnp.concatenate([prev, cur], axis=0)
        conv = cb_ref[...] + ext[halo:] * cw_ref[CONV_WIDTH - 1:CONV_WIDTH, :]
        for j in range(CONV_WIDTH - 1):
            shift = CONV_WIDTH - 1 - j
            conv = conv + pltpu.roll(ext, shift, 0)[halo:] * cw_ref[j:j + 1, :]
        qk = conv * jax.nn.sigmoid(conv)

        G = gate_ref[pl.ds(r0, L), :] + gb_ref[...]
        logf = jnp.minimum(G, 0.0) - jnp.log1p(jnp.exp(-jnp.abs(G)))
        Z = jnp.where(lane < M_HEADS, G, jnp.where(lane < 2 * M_HEADS, logf, 0.0))
        tri = tri_ref[...]
        zh, zm, zl = _split3(Z)
        cum = _dot(tri, zh) + _dot(tri, zm) + _dot(tri, zl)
        ZT = Z.T
        cumT = cum.T

        for h in range(M_HEADS):
            hs = slice(h * HEAD_PAD, (h + 1) * HEAD_PAD)
            q = qk[:, h * HEAD_PAD:(h + 1) * HEAD_PAD]
            k = qk[:, M_PAD + h * HEAD_PAD:M_PAD + (h + 1) * HEAD_PAD] * k_scale
            v = v_ref[pl.ds(r0, L), hs]
            qb = q.astype(_BF)
            kb = k.astype(_BF)
            b_col = cum[:, M_HEADS + h:M_HEADS + h + 1]
            i_col = Z[:, h:h + 1]
            b_row = cumT[M_HEADS + h:M_HEADS + h + 1, :]
            i_row = ZT[h:h + 1, :]
            m_prev = m_ref[h][:, 0:1]
            ct = ct_ref[h]
            n = n_ref[h]

            d = jnp.where(causal, b_col - b_row + i_row, NEG)
            inter = b_col + m_prev
            m_row = jnp.maximum(inter, jnp.max(d, axis=-1, keepdims=True))
            w_intra = jnp.exp(d - m_row)
            w_inter = jnp.exp(inter - m_row)
            s = _dot_nt(qb, kb) * w_intra
            num = _dot(s.astype(_BF), v) + w_inter * _dot(qb, ct.astype(_BF))
            den = jnp.sum(s, axis=-1, keepdims=True) + w_inter * jnp.sum(q * n, axis=-1, keepdims=True)
            hval = num / jnp.maximum(jnp.abs(den), jnp.exp(-m_row))

            b_last = b_col[L - 1:L, :]
            g_row = b_last - b_row + i_row
            g_col = b_last - b_col + i_col
            m_new = jnp.maximum(b_last + m_prev, jnp.max(g_row, axis=-1, keepdims=True))
            a = jnp.exp(b_last + m_prev - m_new)
            wg = jnp.exp(g_col - m_new)
            ct_ref[h] = a * ct + _dot_tn(kb, (wg * v.astype(_F32)).astype(_BF))
            n_ref[h] = a * n + jnp.sum(wg * k, axis=0, keepdims=True)
            m_ref[h] = jnp.broadcast_to(m_new, (1, LANE))

            hm = jax.nn.sigmoid(o_ref[pl.ds(r0, L), hs].astype(_F32)) * hval
            ms = jnp.sum(hm * hm, axis=-1, keepdims=True) * (1.0 / M_HEAD_DIM)
            y_ref[pl.ds(r0, L), hs] = (hm * lax.rsqrt(ms + EPS) * ng_ref[:, hs]).astype(_BF)
        return carry

    lax.fori_loop(0, S // L, chunk, 0)


def _mlstm(qk, v, o, gates, lw, B, S):
    T = B * S

    def seq(width):
        return pl.BlockSpec((S, width), lambda b: (b, 0))

    return pl.pallas_call(
        _mlstm_kernel,
        grid=(B,),
        in_specs=[seq(2 * M_PAD), seq(M_PAD), seq(M_PAD), seq(LANE),
                  _const_spec((CONV_WIDTH, 2 * M_PAD)), _const_spec((1, 2 * M_PAD)), _const_spec((1, LANE)),
                  _const_spec((1, M_PAD)), _const_spec((M_CHUNK, M_CHUNK))],
        out_specs=seq(M_PAD),
        out_shape=jax.ShapeDtypeStruct((T, M_PAD), _BF),
        scratch_shapes=[pltpu.VMEM((M_HEADS, HEAD_PAD, HEAD_PAD), _F32),
                        pltpu.VMEM((M_HEADS, 1, LANE), _F32),
                        pltpu.VMEM((M_HEADS, 1, LANE), _F32)],
        compiler_params=_params(1),
        name="mlstm",
    )(qk, v, o, gates, lw["m_conv_w"], lw["m_conv_b"], lw["m_gate_bias"], lw["m_out_norm"], lw["tri_incl"])


def _mla_kernel(cq_ref, ckr_ref, rope_ref, qg_ref, kvg_ref, wuq_ref, wukv_ref, qhg_ref, khg_ref, ag_ref,
                y_ref, k_scr, v_scr, *head_scr):
    q_scr = head_scr[0:A_HEADS]
    m_scr = head_scr[A_HEADS:2 * A_HEADS]
    acc_scr = head_scr[2 * A_HEADS:3 * A_HEADS]
    S = ckr_ref.shape[0]
    BQ = ATT_BLOCK
    qi = pl.program_id(1)
    lane = lax.broadcasted_iota(jnp.int32, (BQ, LANE), 1)
    swap_lo = lane < A_NOPE + A_ROPE // 2
    inv_qk = 1.0 / A_QK_DIM

    def rope(xn, rt, sg):
        sw = jnp.where(swap_lo, pltpu.roll(xn, LANE - A_ROPE // 2, 1), pltpu.roll(xn, A_ROPE // 2, 1))
        return xn * rt + sw * sg

    @pl.when(qi == 0)
    def _build_kv():
        def kv_block(j, carry):
            r0 = pl.multiple_of(j * BQ, BQ)
            ck = ckr_ref[pl.ds(r0, BQ), :].astype(_F32)
            ckv = ck[:, :LANE]
            k_rope = ck[:, LANE:]
            kvn = (ckv * lax.rsqrt(jnp.mean(ckv * ckv, axis=-1, keepdims=True) + EPS) * kvg_ref[...]).astype(_BF)
            kv = _dot(kvn, wukv_ref[...])
            for h in range(A_HEADS):
                vh = kv[:, A_PAD + h * HEAD_PAD:A_PAD + (h + 1) * HEAD_PAD]
                v_scr[pl.ds(r0, BQ), h * HEAD_PAD:(h + 1) * HEAD_PAD] = jnp.where(lane == A_V_DIM, 1.0, vh).astype(_BF)
            kr = rope(k_rope * khg_ref[...], rope_ref[pl.ds(r0, BQ), 0:LANE], rope_ref[pl.ds(r0, BQ), LANE:2 * LANE])
            ss_rope = jnp.sum(k_rope * k_rope, axis=-1, keepdims=True)
            for h in range(A_HEADS):
                hs = slice(h * HEAD_PAD, (h + 1) * HEAD_PAD)
                kh = kv[:, hs]
                r = lax.rsqrt((jnp.sum(kh * kh, axis=-1, keepdims=True) + ss_rope) * inv_qk + EPS)
                k_scr[j, hs, :] = ((kh * khg_ref[...] + kr) * r).T.astype(_BF)
            return carry

        lax.fori_loop(0, S // BQ, kv_block, 0)

    q0 = pl.multiple_of(qi * BQ, BQ)
    cq = cq_ref[...].astype(_F32)
    qn = (cq * lax.rsqrt(jnp.mean(cq * cq, axis=-1, keepdims=True) + EPS) * qg_ref[...]).astype(_BF)
    qall = _dot(qn, wuq_ref[...])
    g_direct = qhg_ref[0:1, :] * rope_ref[pl.ds(q0, BQ), 0:LANE]
    g_swapped = qhg_ref[1:2, :] * rope_ref[pl.ds(q0, BQ), LANE:2 * LANE]
    scale = (A_QK_DIM ** -0.5) * 1.4426950408889634
    for h in range(A_HEADS):
        hs = slice(h * HEAD_PAD, (h + 1) * HEAD_PAD)
        qh = qall[:, hs]
        qs = qall[:, A_PAD + h * HEAD_PAD:A_PAD + (h + 1) * HEAD_PAD]
        r = lax.rsqrt(jnp.sum(qh * qh, axis=-1, keepdims=True) * inv_qk + EPS) * scale
        q_scr[h][...] = ((qh * g_direct + qs * g_swapped) * r).astype(_BF)
        m_scr[h][...] = jnp.full((BQ, LANE), NEG, _F32)
        acc_scr[h][...] = jnp.zeros((BQ, LANE), _F32)
    col_minus_row = lax.broadcasted_iota(jnp.int32, (BQ, BQ), 1) - lax.broadcasted_iota(jnp.int32, (BQ, BQ), 0)

    def kblock(j, masked):
        k0 = pl.multiple_of(j * BQ, BQ)
        scores = [_dot(q_scr[h][...], k_scr[j, h * HEAD_PAD:(h + 1) * HEAD_PAD, :]) for h in range(A_HEADS)]
        probs = []
        for h in range(A_HEADS):
            s = scores[h]
            if masked:
                s = jnp.where(col_minus_row <= 0, s, NEG)
            m_prev = m_scr[h][...]
            m_new = jnp.maximum(m_prev, jnp.max(s, axis=-1, keepdims=True))
            m_scr[h][...] = m_new
            probs.append((jnp.exp2(m_prev - m_new),
                          jnp.exp2(s - jnp.concatenate([m_new] * (BQ // LANE), axis=1)).astype(_BF)))
        for h in range(A_HEADS):
            hs = slice(h * HEAD_PAD, (h + 1) * HEAD_PAD)
            alpha, pe = probs[h]
            acc_scr[h][...] = alpha * acc_scr[h][...] + _dot(pe, v_scr[pl.ds(k0, BQ), hs])

    def full_block(j, carry):
        kblock(j, False)
        return carry

    def diagonal_block(j, carry):
        kblock(j, True)
        return carry

    lax.fori_loop(0, qi, full_block, 0)
    lax.fori_loop(qi, qi + 1, diagonal_block, 0)

    lo = lane < A_V_DIM
    for p in range(A_HEADS // 2):
        ps = slice(p * LANE, (p + 1) * LANE)
        even = acc_scr[2 * p][...]
        odd = acc_scr[2 * p + 1][...]
        o = jnp.where(lo, even / even[:, A_V_DIM:A_V_DIM + 1], pltpu.roll(odd / odd[:, A_V_DIM:A_V_DIM + 1], A_V_DIM, 1))
        o2 = o * o
        ms_lo = jnp.sum(jnp.where(lo, o2, 0.0), axis=-1, keepdims=True) * (1.0 / A_V_DIM)
        ms_hi = jnp.sum(jnp.where(lo, 0.0, o2), axis=-1, keepdims=True) * (1.0 / A_V_DIM)
        rs = jnp.where(lo, lax.rsqrt(ms_lo + EPS), lax.rsqrt(ms_hi + EPS))
        y_ref[:, ps] = (o * rs * ag_ref[:, ps]).astype(_BF)


def _mla(cq, ckr, rope, lw, B, S):
    T = B * S
    BQ = ATT_BLOCK
    nq = S // BQ
    return pl.pallas_call(
        _mla_kernel,
        grid=(B, nq),
        in_specs=[pl.BlockSpec((BQ, A_Q_RANK), lambda b, q: (b * nq + q, 0)),
                  pl.BlockSpec((S, 2 * LANE), lambda b, q: (b, 0)),
                  pl.BlockSpec((S, 2 * LANE), lambda b, q: (b, 0)),
                  _const_spec((1, A_Q_RANK)), _const_spec((1, A_KV_RANK)),
                  _const_spec((A_Q_RANK, 2 * A_PAD)), _const_spec((A_KV_RANK, 2 * A_PAD)),
                  _const_spec((2, LANE)), _const_spec((1, LANE)), _const_spec((1, A_WIDTH))],
        out_specs=pl.BlockSpec((BQ, A_WIDTH), lambda b, q: (b * nq + q, 0)),
        out_shape=jax.ShapeDtypeStruct((T, A_WIDTH), _BF),
        scratch_shapes=([pltpu.VMEM((nq, A_PAD, BQ), _BF), pltpu.VMEM((S, A_PAD), _BF)]
                        + [pltpu.VMEM((BQ, HEAD_PAD), _BF)] * A_HEADS + [pltpu.VMEM((BQ, LANE), _F32)] * (2 * A_HEADS)),
        compiler_params=_params(2),
        name="mla",
    )(cq, ckr, rope, lw["a_q_norm"], lw["a_kv_norm"], lw["a_w_uq"], lw["a_w_ukv"],
      lw["a_q_head_norm"], lw["a_k_head_norm"], lw["a_out_norm"])


def _outproj_router_kernel(ym_ref, ya_ref, yc_ref, x_ref, wm_ref, wa_ref, wc_ref, g_ref, wr_hi_ref, wr_lo_ref,
                           br_ref, tri_ref, xmid_ref, ri_ref, rw_ref, cnt_ref, *, tiles_per_group):
    tm = x_ref.shape[0]
    i = pl.program_id(0)

    @pl.when(i % tiles_per_group == 0)
    def _():
        cnt_ref[...] = jnp.zeros_like(cnt_ref)

    y = _dot(ym_ref[...], wm_ref[...]) + _dot(ya_ref[...], wa_ref[...]) + _dot(yc_ref[...], wc_ref[...])
    xm = x_ref[...] + y
    xmid_ref[...] = xm
    xn = xm * lax.rsqrt(jnp.mean(xm * xm, axis=-1, keepdims=True) + EPS) * g_ref[...]

    xh, xl = _split2(xn)
    logits = _dot(xh, wr_hi_ref[...]) + _dot(xh, wr_lo_ref[...]) + _dot(xl, wr_hi_ref[...])
    biased = logits + br_ref[...]
    lane = lax.broadcasted_iota(jnp.int32, (tm, LANE), 1)
    lane_f = lane.astype(_F32)

    def lane_max(a):
        return jnp.max(a, axis=-1, keepdims=True)

    def lane_sum(a):
        return jnp.sum(a, axis=-1, keepdims=True)

    def first_argmax(a, amax):
        return jnp.min(jnp.where(a == amax, lane_f, float(LANE)), axis=-1, keepdims=True).astype(jnp.int32)

    gmask = lane < N_GROUPS
    gl = jnp.where(gmask, logits, NEG)
    gexp = jnp.where(gmask, jnp.exp(gl - lane_max(gl)), 0.0)
    gsc = jnp.where(gmask, biased, NEG)
    g_sel = first_argmax(gsc, lane_max(gsc))
    g_gate = lane_sum(jnp.where(lane == g_sel, gexp, 0.0)) / lane_sum(gexp)

    e_lo = N_GROUPS + EXPERTS_PER_GROUP * g_sel
    emask = jnp.logical_and(lane >= e_lo, lane < e_lo + EXPERTS_PER_GROUP)
    el = jnp.where(emask, logits, NEG)
    eexp = jnp.where(emask, jnp.exp(el - lane_max(el)), 0.0)
    esc = jnp.where(emask, biased, NEG)
    i1 = first_argmax(esc, lane_max(esc))
    esc2 = jnp.where(lane == i1, NEG, esc)
    i2 = first_argmax(esc2, lane_max(esc2))
    oh1 = lane == i1
    oh2 = lane == i2
    p1 = lane_sum(jnp.where(oh1, eexp, 0.0))
    p2 = lane_sum(jnp.where(oh2, eexp, 0.0))
    w1 = g_gate * p1 / (p1 + p2)
    w2 = g_gate * p2 / (p1 + p2)

    cnt = jnp.where(oh1, 1.0, 0.0) + jnp.where(oh2, 1.0, 0.0)
    base = cnt_ref[0] + _dot(tri_ref[...], cnt.astype(_BF))
    rank1 = lane_sum(jnp.where(oh1, base, 0.0)).astype(jnp.int32)
    rank2 = lane_sum(jnp.where(oh2, base, 0.0)).astype(jnp.int32)
    cnt_ref[0] = cnt_ref[0] + jnp.sum(cnt, axis=0, keepdims=True)

    code1 = (i1 - N_GROUPS) * RANK_SPAN + rank1
    code2 = (i2 - N_GROUPS) * RANK_SPAN + rank2
    ri_ref[...] = jnp.where(lane == 0, code1, jnp.where(lane == 1, code2, 0))
    rw_ref[...] = jnp.where(lane == 0, w1, jnp.where(lane == 1, w2, 0.0))


def _outproj_router(ym, ya, yc, x, lw, tm, tg):
    T = x.shape[0]
    tiles_per_group = tg // tm

    def tok(width):
        return pl.BlockSpec((tm, width), lambda i: (i, 0))

    return pl.pallas_call(
        functools.partial(_outproj_router_kernel, tiles_per_group=tiles_per_group),
        grid=(T // tm,),
        in_specs=[tok(M_PAD), tok(A_WIDTH), tok(C_WIDTH), tok(D_MODEL),
                  _const_spec((M_PAD, D_MODEL)), _const_spec((A_WIDTH, D_MODEL)), _const_spec((C_WIDTH, D_MODEL)),
                  _const_spec((1, D_MODEL)), _const_spec((D_MODEL, LANE)), _const_spec((D_MODEL, LANE)),
                  _const_spec((1, LANE)), _const_spec((tm, tm))],
        out_specs=[tok(D_MODEL), tok(LANE), tok(LANE),
                   pl.BlockSpec((1, 1, LANE), lambda i: (i // tiles_per_group, 0, 0))],
        out_shape=[jax.ShapeDtypeStruct((T, D_MODEL), _F32),
                   jax.ShapeDtypeStruct((T, LANE), jnp.int32), jax.ShapeDtypeStruct((T, LANE), _F32),
                   jax.ShapeDtypeStruct((T // tg, 1, LANE), _F32)],
        compiler_params=_params(1),
        name="outproj_router",
    )(ym, ya, yc, x, lw["w_out_m"], lw["w_out_a"], lw["w_out_c"], lw["ffn_norm"], lw["w_router_hi"],
      lw["w_router_lo"], lw["b_router"], lw["tri_strict"])


SLAB = 8
MOE_CHUNK = 160
MOE_ROWS = 8
CONV_ROWS = 256
INVERT_UNROLL = 8


def _moe_kernel(off_ref, code_ref, w_ref, xmid_ref, g_ref, w1_ref, w3_ref, w2_ref, out_ref,
                xs_ref, acc_ref, g_buf, y_buf, inv_ref):
    i = pl.program_id(0)
    e = pl.program_id(1)
    tg = xmid_ref.shape[0]
    CH = MOE_CHUNK

    def slab_rows(r0, j, n):
        return pl.ds(pl.multiple_of(r0 * SLAB, SLAB) + j, n, stride=SLAB)

    def slab(row):
        return pl.ds(pl.multiple_of(row * SLAB, SLAB), SLAB)

    @pl.when(jnp.logical_and(i == 0, e == 0))
    def _():
        g_buf[...] = jnp.zeros_like(g_buf)

    @pl.when(e == 0)
    def _prepare_group():
        def convert(rb, carry):
            r0 = pl.multiple_of(rb * CONV_ROWS, CONV_ROWS)
            xm = xmid_ref[pl.ds(r0, CONV_ROWS), :]
            xn = xm * lax.rsqrt(jnp.mean(xm * xm, axis=-1, keepdims=True) + EPS) * g_ref[...]
            for j in range(SLAB):
                xs_ref[slab_rows(r0, j, CONV_ROWS), :] = xn[:, j * LANE:(j + 1) * LANE]
                acc_ref[slab_rows(r0, j, CONV_ROWS), :] = xm[:, j * LANE:(j + 1) * LANE]
            return carry

        lax.fori_loop(0, tg // CONV_ROWS, convert, 0)
        acc_ref[slab(tg), :] = jnp.zeros((SLAB, LANE), _F32)

        def invert(it, carry):
            a0 = it * INVERT_UNROLL
            slots = []
            for k in range(INVERT_UNROLL):
                code = code_ref[0, 0, a0 + k]
                expert = lax.shift_right_logical(code, RANK_BITS)
                slots.append(off_ref[i * (N_EXPERTS + 1) + expert] + (code & (RANK_SPAN - 1)))
            for k, pos in enumerate(slots):
                inv_ref[pos] = a0 + k
            return carry

        lax.fori_loop(0, 2 * tg // INVERT_UNROLL, invert, 0)
        for r in range(MOE_ROWS):
            inv_ref[2 * tg + r] = 0

    base = off_ref[i * (N_EXPERTS + 1) + e]
    end = off_ref[i * (N_EXPERTS + 1) + e + 1]

    def chunk(c, carry):
        b0 = base + c * CH
        n_rows = jnp.minimum(end - b0, CH)
        n_full = n_rows // MOE_ROWS

        def gather(it, cc):
            p0 = b0 + it * MOE_ROWS
            d0 = pl.multiple_of(it * (MOE_ROWS * SLAB), MOE_ROWS * SLAB)
            for r in range(MOE_ROWS):
                tok = lax.shift_right_logical(inv_ref[p0 + r], 1)
                g_buf[pl.ds(d0 + r * SLAB, SLAB), :] = xs_ref[slab(tok), :]
            return cc

        lax.fori_loop(0, (n_rows + MOE_ROWS - 1) // MOE_ROWS, gather, 0)
        x = jnp.concatenate([g_buf[slab_rows(0, j, CH), :] for j in range(SLAB)], axis=1).astype(_BF)
        hid = (jax.nn.silu(_dot(x, w1_ref[0])) * _dot(x, w3_ref[0])).astype(_BF)
        y = _dot(hid, w2_ref[0])
        for j in range(SLAB):
            y_buf[slab_rows(0, j, CH), :] = y[:, j * LANE:(j + 1) * LANE]

        def scatter(it, masked):
            p0 = b0 + it * MOE_ROWS
            d0 = pl.multiple_of(it * (MOE_ROWS * SLAB), MOE_ROWS * SLAB)
            updates = []
            for r in range(MOE_ROWS):
                a = inv_ref[p0 + r]
                tok = lax.shift_right_logical(a, 1)
                wgt = w_ref[0, 0, a]
                if masked:
                    ok = p0 + r < end
                    tok = jnp.where(ok, tok, tg)
                    wgt = jnp.where(ok, wgt, 0.0)
                updates.append((tok, acc_ref[slab(tok), :] + wgt * y_buf[pl.ds(d0 + r * SLAB, SLAB), :]))
            for tok, val in updates:
                acc_ref[slab(tok), :] = val

        def scatter_full(it, cc):
            scatter(it, False)
            return cc

        lax.fori_loop(0, n_full, scatter_full, 0)

        @pl.when(n_rows > n_full * MOE_ROWS)
        def _():
            scatter(n_full, True)

        return carry

    lax.fori_loop(0, (end - base + CH - 1) // CH, chunk, 0)

    @pl.when(e == N_EXPERTS - 1)
    def _finish_group():
        def convert(rb, carry):
            r0 = pl.multiple_of(rb * CONV_ROWS, CONV_ROWS)
            for j in range(SLAB):
                out_ref[pl.ds(r0, CONV_ROWS), j * LANE:(j + 1) * LANE] = acc_ref[slab_rows(r0, j, CONV_ROWS), :]
            return carry

        lax.fori_loop(0, tg // CONV_ROWS, convert, 0)


def _moe(xmid, ri, rw, counts, lw, tg):
    T = xmid.shape[0]
    ng = T // tg
    cnt = counts[:, 0, N_GROUPS:N_GROUPS + N_EXPERTS].astype(jnp.int32)
    ends = jnp.cumsum(cnt, axis=1)
    offs = jnp.concatenate([jnp.zeros((ng, 1), jnp.int32), ends], axis=1).reshape(ng * (N_EXPERTS + 1))
    per_group = lambda a: a.reshape(ng, 1, 2 * tg)
    smem = pl.BlockSpec((1, 1, 2 * tg), lambda i, e, off: (i, 0, 0), memory_space=pltpu.SMEM)
    grid_spec = pltpu.PrefetchScalarGridSpec(
        num_scalar_prefetch=1,
        grid=(ng, N_EXPERTS),
        in_specs=[smem, smem,
                  pl.BlockSpec((tg, D_MODEL), lambda i, e, off: (i, 0), pipeline_mode=pl.Buffered(1)),
                  pl.BlockSpec((1, D_MODEL), lambda i, e, off: (0, 0)),
                  pl.BlockSpec((1, D_MODEL, D_EXPERT), lambda i, e, off: (e, 0, 0)),
                  pl.BlockSpec((1, D_MODEL, D_EXPERT), lambda i, e, off: (e, 0, 0)),
                  pl.BlockSpec((1, D_EXPERT, D_MODEL), lambda i, e, off: (e, 0, 0))],
        out_specs=pl.BlockSpec((tg, D_MODEL), lambda i, e, off: (i, 0)),
        scratch_shapes=[pltpu.VMEM((tg * SLAB, LANE), _F32), pltpu.VMEM(((tg + 1) * SLAB, LANE), _F32),
                        pltpu.VMEM((MOE_CHUNK * SLAB, LANE), _F32), pltpu.VMEM((MOE_CHUNK * SLAB, LANE), _F32),
                        pltpu.SMEM((2 * tg + MOE_ROWS,), jnp.int32)],
    )
    return pl.pallas_call(
        _moe_kernel,
        grid_spec=grid_spec,
        out_shape=jax.ShapeDtypeStruct((T, D_MODEL), _F32),
        compiler_params=_params(2),
        name="moe",
    )(offs, per_group(ri[:, 0:2]), per_group(rw[:, 0:2]), xmid, lw["ffn_norm"], lw["w1"], lw["w3"], lw["w2"])


def _pad_heads(w, n_heads, dim):
    lead = w.shape[:-1]
    w = w.reshape(lead + (n_heads, dim))
    w = jnp.pad(w, [(0, 0)] * len(lead) + [(0, 0), (0, HEAD_PAD - dim)])
    return w.reshape(lead + (n_heads * HEAD_PAD,))


def _swap_rope_halves(w):
    lead = w.shape[:-1]
    w = w.reshape(lead + (-1, HEAD_PAD))
    half = A_ROPE // 2
    w = jnp.concatenate([w[..., :A_NOPE], w[..., A_NOPE + half:A_NOPE + A_ROPE], w[..., A_NOPE:A_NOPE + half],
                         w[..., A_NOPE + A_ROPE:]], axis=-1)
    return w.reshape(lead + (-1,))


def _pack_layer(p, l, tm):
    w_in = p["w_in"][l]
    sizes = (M_WIDTH, M_WIDTH, M_WIDTH, M_WIDTH, M_HEADS, M_HEADS, A_Q_RANK, A_KV_RANK, A_ROPE, C_WIDTH, C_WIDTH)
    cols, off = [], 0
    for s in sizes:
        cols.append(w_in[:, off:off + s])
        off += s
    mq, mk, mv, mo, mi, mf, cq, ckv, krope, cu, cv = cols
    zeros = lambda n: jnp.zeros((D_MODEL, n), _F32)
    w_all = jnp.concatenate([
        _pad_heads(mq, M_HEADS, M_HEAD_DIM), _pad_heads(mk, M_HEADS, M_HEAD_DIM),
        _pad_heads(mv, M_HEADS, M_HEAD_DIM), _pad_heads(mo, M_HEADS, M_HEAD_DIM),
        mi, mf, zeros(LANE - 2 * M_HEADS),
        cq, ckv, zeros(A_NOPE), krope, zeros(LANE - A_NOPE - A_ROPE), cu, cv], axis=1).astype(_BF)

    g = p["mix_out_norm"][l]
    w_out = p["w_out"][l]
    w_out_m = jnp.pad(w_out[:M_WIDTH].reshape(M_HEADS, M_HEAD_DIM, D_MODEL),
                      ((0, 0), (0, HEAD_PAD - M_HEAD_DIM), (0, 0))).reshape(M_PAD, D_MODEL)

    w_uq = _pad_heads(p["a_w_uq"][l], A_HEADS, A_QK_DIM)
    w_uq = jnp.concatenate([w_uq, _swap_rope_halves(w_uq)], axis=1)
    w_ukv = p["a_w_ukv"][l].reshape(A_KV_RANK, A_HEADS, A_NOPE + A_V_DIM)
    w_uk = _pad_heads(w_ukv[:, :, :A_NOPE].reshape(A_KV_RANK, A_HEADS * A_NOPE), A_HEADS, A_NOPE)
    w_uv = _pad_heads(w_ukv[:, :, A_NOPE:].reshape(A_KV_RANK, A_WIDTH), A_HEADS, A_V_DIM)
    q_head_gain = jnp.pad(p["a_q_head_norm"][l], (0, HEAD_PAD - A_QK_DIM)).reshape(1, HEAD_PAD)

    w_router = jnp.concatenate([p["w_group"][l], p["w_expert"][l],
                                jnp.zeros((D_MODEL, LANE - N_GROUPS - N_EXPERTS), _F32)], axis=1)
    wr_hi = w_router.astype(_BF)
    wr_lo = (w_router - wr_hi.astype(_F32)).astype(_BF)
    b_router = jnp.concatenate([p["b_group"][l], p["b_expert"][l],
                                jnp.zeros((LANE - N_GROUPS - N_EXPERTS,), _F32)]).reshape(1, LANE)

    tril = jnp.tril(jnp.ones((C_CHUNK, C_CHUNK), _F32))
    lane_pad = lambda a: jnp.pad(a, (0, HEAD_PAD - a.shape[0])).reshape(1, HEAD_PAD)
    return {
        "attn_norm": p["attn_norm"][l].reshape(1, D_MODEL),
        "w_in": w_all,
        "c_v_norm": p["c_v_norm"][l].reshape(1, C_WIDTH),
        "c_w_s": (p["c_w_s"][l] * tril).astype(_BF),
        "c_b_s": jnp.repeat(p["c_b_s"][l].T, C_GROUP_DIM, axis=1),
        "c_out_norm": g[M_WIDTH + A_WIDTH:].reshape(1, C_WIDTH),
        "ones_bd": jnp.kron(jnp.eye(C_GROUPS, dtype=_F32), jnp.ones((C_GROUP_DIM, C_GROUP_DIM), _F32)).astype(_BF),
        "m_conv_w": jnp.concatenate([_pad_heads(p["m_conv_w"][l][:, :M_WIDTH], M_HEADS, M_HEAD_DIM),
                                     _pad_heads(p["m_conv_w"][l][:, M_WIDTH:], M_HEADS, M_HEAD_DIM)], axis=1),
        "m_conv_b": jnp.concatenate([_pad_heads(p["m_conv_b"][l][:M_WIDTH], M_HEADS, M_HEAD_DIM),
                                     _pad_heads(p["m_conv_b"][l][M_WIDTH:], M_HEADS, M_HEAD_DIM)]).reshape(1, 2 * M_PAD),
        "m_gate_bias": jnp.pad(p["m_gate_bias"][l], (0, LANE - 2 * M_HEADS)).reshape(1, LANE),
        "m_out_norm": _pad_heads(g[:M_WIDTH], M_HEADS, M_HEAD_DIM).reshape(1, M_PAD),
        "tri_incl": tril.astype(_BF),
        "a_q_norm": p["a_q_norm"][l].reshape(1, A_Q_RANK),
        "a_kv_norm": p["a_kv_norm"][l].reshape(1, A_KV_RANK),
        "a_w_uq": w_uq.astype(_BF),
        "a_w_ukv": jnp.concatenate([w_uk, w_uv], axis=1).astype(_BF),
        "a_q_head_norm": jnp.concatenate([q_head_gain, _swap_rope_halves(q_head_gain)], axis=0),
        "a_k_head_norm": lane_pad(p["a_k_head_norm"][l]),
        "a_out_norm": g[M_WIDTH:M_WIDTH + A_WIDTH].reshape(1, A_WIDTH),
        "w_out_m": w_out_m.astype(_BF),
        "w_out_a": w_out[M_WIDTH:M_WIDTH + A_WIDTH].astype(_BF),
        "w_out_c": w_out[M_WIDTH + A_WIDTH:].astype(_BF),
        "ffn_norm": p["ffn_norm"][l].reshape(1, D_MODEL),
        "w_router_hi": wr_hi,
        "w_router_lo": wr_lo,
        "b_router": b_router,
        "tri_strict": jnp.tril(jnp.ones((tm, tm), _F32), -1).astype(_BF),
        "w1": p["w1"][l].astype(_BF),
        "w3": p["w3"][l].astype(_BF),
        "w2": p["w2"][l].astype(_BF),
    }


def _layer(x, rope, lw, B, S, tm, tg):
    qk, v, o, gates, cq, ckr, yc = _inproj(x, lw, tm)
    ym = _mlstm(qk, v, o, gates, lw, B, S)
    ya = _mla(cq, ckr, rope, lw, B, S)
    xmid, ri, rw, counts = _outproj_router(ym, ya, yc, x, lw, tm, tg)
    return _moe(xmid, ri, rw, counts, lw, tg)


def kernel(x, positions, attn_norm, w_in, m_conv_w, m_conv_b, m_gate_bias, a_q_norm, a_kv_norm, a_w_uq, a_w_ukv, a_q_head_norm, a_k_head_norm, c_v_norm, c_w_s, c_b_s, mix_out_norm, w_out, ffn_norm, w_group, b_group, w_expert, b_expert, w1, w3, w2):
    B, S, D = x.shape
    assert D == D_MODEL and S % ATT_BLOCK == 0
    T = B * S
    tm = _token_tile(T)
    tg = _moe_group(T, tm)
    p = dict(attn_norm=attn_norm, w_in=w_in, m_conv_w=m_conv_w, m_conv_b=m_conv_b, m_gate_bias=m_gate_bias,
             a_q_norm=a_q_norm, a_kv_norm=a_kv_norm, a_w_uq=a_w_uq, a_w_ukv=a_w_ukv, a_q_head_norm=a_q_head_norm,
             a_k_head_norm=a_k_head_norm, c_v_norm=c_v_norm, c_w_s=c_w_s, c_b_s=c_b_s, mix_out_norm=mix_out_norm,
             w_out=w_out, ffn_norm=ffn_norm, w_group=w_group, b_group=b_group, w_expert=w_expert,
             b_expert=b_expert, w1=w1, w3=w3, w2=w2)
    rope = _rope_tables(positions)
    xt = x.reshape(T, D)
    for l in range(w_in.shape[0]):
        xt = _layer(xt, rope, _pack_layer(p, l, tm), B, S, tm, tg)
    return xt.reshape(B, S, D)
```

```python
import functools

import jax
import jax.numpy as jnp
from jax import lax
from jax.experimental import pallas as pl
from jax.experimental.pallas import tpu as pltpu

D_MODEL = 1024
M_HEADS = 4
M_HEAD_DIM = 96
M_WIDTH = M_HEADS * M_HEAD_DIM
M_CHUNK = 128
CONV_WIDTH = 4
A_HEADS = 6
A_NOPE = 64
A_ROPE = 32
A_QK_DIM = A_NOPE + A_ROPE
A_V_DIM = 64
A_WIDTH = A_HEADS * A_V_DIM
A_Q_RANK = 256
A_KV_RANK = 128
ROPE_THETA = 10000.0
C_GROUPS = 4
C_GROUP_DIM = 64
C_WIDTH = C_GROUPS * C_GROUP_DIM
C_CHUNK = 128
N_GROUPS = 4
EXPERTS_PER_GROUP = 8
N_EXPERTS = N_GROUPS * EXPERTS_PER_GROUP
D_EXPERT = 256
EPS = 1e-6

LANE = 128
HEAD_PAD = LANE
M_PAD = M_HEADS * HEAD_PAD
A_PAD = A_HEADS * HEAD_PAD
VMEM_LIMIT = 48 * 1024 * 1024

_COL_QK = (0, 2 * M_PAD)
_COL_V = (_COL_QK[1], _COL_QK[1] + M_PAD)
_COL_O = (_COL_V[1], _COL_V[1] + M_PAD)
_COL_G = (_COL_O[1], _COL_O[1] + LANE)
_COL_CQ = (_COL_G[1], _COL_G[1] + A_Q_RANK)
_COL_CKR = (_COL_CQ[1], _COL_CQ[1] + 2 * LANE)
_COL_CU = (_COL_CKR[1], _COL_CKR[1] + C_WIDTH)
_COL_CV = (_COL_CU[1], _COL_CU[1] + C_WIDTH)
D_IN_PAD = _COL_CV[1]

ATT_BLOCK = 256
RANK_SPAN = 2048
NEG = -1e30

_BF = jnp.bfloat16
_F32 = jnp.float32


def _dot(a, b):
    return jnp.dot(a, b, preferred_element_type=_F32)


def _dot_nt(a, b):
    return lax.dot_general(a, b, (((1,), (1,)), ((), ())), preferred_element_type=_F32)


def _dot_tn(a, b):
    return lax.dot_general(a, b, (((0,), (0,)), ((), ())), preferred_element_type=_F32)


def _split2(a):
    hi = a.astype(_BF)
    lo = (a - hi.astype(_F32)).astype(_BF)
    return hi, lo


def _split3(a):
    hi = a.astype(_BF)
    r1 = a - hi.astype(_F32)
    mid = r1.astype(_BF)
    lo = (r1 - mid.astype(_F32)).astype(_BF)
    return hi, mid, lo


def _group_sum(a, ones_bd):
    hi, lo = _split2(a)
    return _dot(hi, ones_bd) + _dot(lo, ones_bd)


def _params(n_axes, flags=None):
    return pltpu.CompilerParams(dimension_semantics=("arbitrary",) * n_axes, vmem_limit_bytes=VMEM_LIMIT,
                                flags=flags)


def _const_spec(shape):
    nd = len(shape)
    return pl.BlockSpec(shape, lambda *_: (0,) * nd)


def _token_tile(T):
    for tm in (1024, 512, 256, 128):
        if T % tm == 0:
            return tm
    raise ValueError(f"token count {T} must be a multiple of 128")


def _moe_group(T, tm):
    for tg in (2048, 1024, 512, 256, 128):
        if T % tg == 0 and tg % tm == 0:
            return tg
    raise ValueError(f"no MoE token group for {T} tokens with router tile {tm}")


def _rope_kernel(pos_ref, invf_ref, sgn_ref, out_ref):
    S = pos_ref.shape[-1]
    pos = pos_ref[0].astype(_F32)
    ang = invf_ref[...] * pos
    c = jnp.cos(ang)
    s = jnp.sin(ang) * sgn_ref[...]
    for j in range(S // LANE):
        out_ref[0, j * LANE:(j + 1) * LANE, 0:LANE] = c[:, j * LANE:(j + 1) * LANE].T
        out_ref[0, j * LANE:(j + 1) * LANE, LANE:2 * LANE] = s[:, j * LANE:(j + 1) * LANE].T


def _rope_tables(positions):
    B, S = positions.shape
    inv_freq = 1.0 / (ROPE_THETA ** (jnp.arange(0, A_ROPE, 2, dtype=_F32) / A_ROPE))
    half = A_ROPE // 2
    invf = jnp.zeros((LANE,), _F32).at[A_NOPE:A_NOPE + A_ROPE].set(jnp.concatenate([inv_freq, inv_freq]))
    sgn = jnp.zeros((LANE,), _F32).at[A_NOPE:A_NOPE + half].set(-1.0).at[A_NOPE + half:A_NOPE + A_ROPE].set(1.0)
    return pl.pallas_call(
        _rope_kernel,
        grid=(B,),
        in_specs=[pl.BlockSpec((1, 1, S), lambda b: (b, 0, 0)), _const_spec((LANE, 1)), _const_spec((LANE, 1))],
        out_specs=pl.BlockSpec((1, S, 2 * LANE), lambda b: (b, 0, 0)),
        out_shape=jax.ShapeDtypeStruct((B, S, 2 * LANE), _F32),
        compiler_params=_params(1),
        name="rope_tables",
    )(positions.reshape(B, 1, S), invf.reshape(LANE, 1), sgn.reshape(LANE, 1)).reshape(B * S, 2 * LANE)


def _inproj_kernel(x_ref, g_ref, w_ref, cvg_ref, ws_ref, bs_ref, cg_ref, ones_ref,
                   qk_ref, v_ref, o_ref, gate_ref, cq_ref, ckr_ref, yc_ref):
    tm = x_ref.shape[0]
    x = x_ref[...]
    h = (x * lax.rsqrt(jnp.mean(x * x, axis=-1, keepdims=True) + EPS) * g_ref[...]).astype(_BF)

    def proj(col):
        return _dot(h, w_ref[:, col[0]:col[1]])

    qk_ref[...] = proj(_COL_QK).astype(_BF)
    v_ref[...] = proj(_COL_V).astype(_BF)
    o_ref[...] = proj(_COL_O).astype(_BF)
    gate_ref[...] = proj(_COL_G)
    cq_ref[...] = proj(_COL_CQ).astype(_BF)
    ckr_ref[...] = proj(_COL_CKR).astype(_BF)

    ones_bd = ones_ref[...]
    u = jax.nn.gelu(proj(_COL_CU))
    v = jax.nn.gelu(proj(_COL_CV))
    vn = v * lax.rsqrt(_group_sum(v * v, ones_bd) * (1.0 / C_GROUP_DIM) + EPS) * cvg_ref[...]
    vnb = vn.astype(_BF)
    lane = lax.broadcasted_iota(jnp.int32, (C_CHUNK, C_WIDTH), 1)
    for c in range(tm // C_CHUNK):
        rows = slice(c * C_CHUNK, (c + 1) * C_CHUNK)
        vc = vnb[rows]
        mixed = _dot(ws_ref[0], vc)
        for g in range(1, C_GROUPS):
            mixed = jnp.where(lane >= g * C_GROUP_DIM, _dot(ws_ref[g], vc), mixed)
        hc = u[rows] * (mixed + bs_ref[...])
        y = hc * lax.rsqrt(_group_sum(hc * hc, ones_bd) * (1.0 / C_GROUP_DIM) + EPS) * cg_ref[...]
        yc_ref[rows, :] = y.astype(_BF)


def _inproj(x, lw, tm):
    T = x.shape[0]

    def tok(width):
        return pl.BlockSpec((tm, width), lambda i: (i, 0))

    outs = [(2 * M_PAD, _BF), (M_PAD, _BF), (M_PAD, _BF), (LANE, _F32), (A_Q_RANK, _BF), (2 * LANE, _BF), (C_WIDTH, _BF)]
    return pl.pallas_call(
        _inproj_kernel,
        grid=(T // tm,),
        in_specs=[tok(D_MODEL), _const_spec((1, D_MODEL)), _const_spec((D_MODEL, D_IN_PAD)),
                  _const_spec((1, C_WIDTH)), _const_spec((C_GROUPS, C_CHUNK, C_CHUNK)),
                  _const_spec((C_CHUNK, C_WIDTH)), _const_spec((1, C_WIDTH)), _const_spec((C_WIDTH, C_WIDTH))],
        out_specs=[tok(w) for w, _ in outs],
        out_shape=[jax.ShapeDtypeStruct((T, w), dt) for w, dt in outs],
        compiler_params=_params(1),
        name="inproj_gating",
    )(x, lw["attn_norm"], lw["w_in"], lw["c_v_norm"], lw["c_w_s"], lw["c_b_s"], lw["c_out_norm"], lw["ones_bd"])


def _mlstm_kernel(qk_ref, v_ref, o_ref, gate_ref, cw_ref, cb_ref, gb_ref, ng_ref, tri_ref,
                  y_ref, ct_ref, n_ref, m_ref):
    S = qk_ref.shape[0]
    L = M_CHUNK
    ct_ref[...] = jnp.zeros_like(ct_ref)
    n_ref[...] = jnp.zeros_like(n_ref)
    m_ref[...] = jnp.zeros_like(m_ref)
    lane = lax.broadcasted_iota(jnp.int32, (L, LANE), 1)
    causal = lax.broadcasted_iota(jnp.int32, (L, L), 0) >= lax.broadcasted_iota(jnp.int32, (L, L), 1)
    k_scale = M_HEAD_DIM ** -0.5
    halo = 16

    def chunk(c, carry):
        r0 = pl.multiple_of(c * L, L)
        cur = qk_ref[pl.ds(r0, L), :].astype(_F32)
        p0 = pl.multiple_of(jnp.maximum(r0 - halo, 0), halo)
        prev = jnp.where(c > 0, qk_ref[pl.ds(p0, halo), :].astype(_F32), 0.0)
        ext = jnp.concatenate([prev, cur], axis=0)
        conv = cb_ref[...] + ext[halo:] * cw_ref[CONV_WIDTH - 1:CONV_WIDTH, :]
        for j in range(CONV_WIDTH - 1):
            shift = CONV_WIDTH - 1 - j
            conv = conv + pltpu.roll(ext, shift, 0)[halo:] * cw_ref[j:j + 1, :]
        qk = conv * jax.nn.sigmoid(conv)

        G = gate_ref[pl.ds(r0, L), :] + gb_ref[...]
        logf = jnp.minimum(G, 0.0) - jnp.log1p(jnp.exp(-jnp.abs(G)))
        Z = jnp.where(lane < M_HEADS, G, jnp.where(lane < 2 * M_HEADS, logf, 0.0))
        tri = tri_ref[...]
        zh, zm, zl = _split3(Z)
        cum = _dot(tri, zh) + _dot(tri, zm) + _dot(tri, zl)
        ZT = Z.T
        cumT = cum.T

        for h in range(M_HEADS):
            hs = slice(h * HEAD_PAD, (h + 1) * HEAD_PAD)
            q = qk[:, h * HEAD_PAD:(h + 1) * HEAD_PAD]
            k = qk[:, M_PAD + h * HEAD_PAD:M_PAD + (h + 1) * HEAD_PAD] * k_scale
            v = v_ref[pl.ds(r0, L), hs]
            qb = q.astype(_BF)
            kb = k.astype(_BF)
            b_col = cum[:, M_HEADS + h:M_HEADS + h + 1]
            i_col = Z[:, h:h + 1]
            b_row = cumT[M_HEADS + h:M_HEADS + h + 1, :]
            i_row = ZT[h:h + 1, :]
            m_prev = m_ref[h][:, 0:1]
            ct = ct_ref[h]
            n = n_ref[h]

            d = jnp.where(causal, b_col - b_row + i_row, NEG)
            inter = b_col + m_prev
            m_row = jnp.maximum(inter, jnp.max(d, axis=-1, keepdims=True))
            w_intra = jnp.exp(d - m_row)
            w_inter = jnp.exp(inter - m_row)
            s = _dot_nt(qb, kb) * w_intra
            num = _dot(s.astype(_BF), v) + w_inter * _dot(qb, ct.astype(_BF))
            den = jnp.sum(s, axis=-1, keepdims=True) + w_inter * jnp.sum(q * n, axis=-1, keepdims=True)
            hval = num / jnp.maximum(jnp.abs(den), jnp.exp(-m_row))

            b_last = b_col[L - 1:L, :]
            g_row = b_last - b_row + i_row
            g_col = b_last - b_col + i_col
            m_new = jnp.maximum(b_last + m_prev, jnp.max(g_row, axis=-1, keepdims=True))
            a = jnp.exp(b_last + m_prev - m_new)
            wg = jnp.exp(g_col - m_new)
            ct_ref[h] = a * ct + _dot_tn(kb, (wg * v.astype(_F32)).astype(_BF))
            n_ref[h] = a * n + jnp.sum(wg * k, axis=0, keepdims=True)
            m_ref[h] = jnp.broadcast_to(m_new, (1, LANE))

            hm = jax.nn.sigmoid(o_ref[pl.ds(r0, L), hs].astype(_F32)) * hval
            ms = jnp.sum(hm * hm, axis=-1, keepdims=True) * (1.0 / M_HEAD_DIM)
            y_ref[pl.ds(r0, L), hs] = (hm * lax.rsqrt(ms + EPS) * ng_ref[:, hs]).astype(_BF)
        return carry

    lax.fori_loop(0, S // L, chunk, 0)


def _mlstm(qk, v, o, gates, lw, B, S):
    T = B * S

    def seq(width):
        return pl.BlockSpec((S, width), lambda b: (b, 0))

    return pl.pallas_call(
        _mlstm_kernel,
        grid=(B,),
        in_specs=[seq(2 * M_PAD), seq(M_PAD), seq(M_PAD), seq(LANE),
                  _const_spec((CONV_WIDTH, 2 * M_PAD)), _const_spec((1, 2 * M_PAD)), _const_spec((1, LANE)),
                  _const_spec((1, M_PAD)), _const_spec((M_CHUNK, M_CHUNK))],
        out_specs=seq(M_PAD),
        out_shape=jax.ShapeDtypeStruct((T, M_PAD), _BF),
        scratch_shapes=[pltpu.VMEM((M_HEADS, HEAD_PAD, HEAD_PAD), _F32),
                        pltpu.VMEM((M_HEADS, 1, LANE), _F32),
                        pltpu.VMEM((M_HEADS, 1, LANE), _F32)],
        compiler_params=_params(1),
        name="mlstm",
    )(qk, v, o, gates, lw["m_conv_w"], lw["m_conv_b"], lw["m_gate_bias"], lw["m_out_norm"], lw["tri_incl"])


def _mla_kernel(cq_ref, ckr_ref, rope_ref, qg_ref, kvg_ref, wuq_ref, wukv_ref, qhg_ref, khg_ref, ag_ref,
                y_ref, k_scr, v_scr, *head_scr):
    q_scr = head_scr[0:A_HEADS]
    m_scr = head_scr[A_HEADS:2 * A_HEADS]
    acc_scr = head_scr[2 * A_HEADS:3 * A_HEADS]
    S = ckr_ref.shape[0]
    BQ = ATT_BLOCK
    qi = pl.program_id(1)
    lane = lax.broadcasted_iota(jnp.int32, (BQ, LANE), 1)
    swap_lo = lane < A_NOPE + A_ROPE // 2
    inv_qk = 1.0 / A_QK_DIM

    def rope(xn, rt, sg):
        sw = jnp.where(swap_lo, pltpu.roll(xn, LANE - A_ROPE // 2, 1), pltpu.roll(xn, A_ROPE // 2, 1))
        return xn * rt + sw * sg

    @pl.when(qi == 0)
    def _build_kv():
        def kv_block(j, carry):
            r0 = pl.multiple_of(j * BQ, BQ)
            ck = ckr_ref[pl.ds(r0, BQ), :].astype(_F32)
            ckv = ck[:, :LANE]
            k_rope = ck[:, LANE:]
            kvn = (ckv * lax.rsqrt(jnp.mean(ckv * ckv, axis=-1, keepdims=True) + EPS) * kvg_ref[...]).astype(_BF)
            kv = _dot(kvn, wukv_ref[...])
            for h in range(A_HEADS):
                vh = kv[:, A_PAD + h * HEAD_PAD:A_PAD + (h + 1) * HEAD_PAD]
                v_scr[pl.ds(r0, BQ), h * HEAD_PAD:(h + 1) * HEAD_PAD] = jnp.where(lane == A_V_DIM, 1.0, vh).astype(_BF)
            kr = rope(k_rope * khg_ref[...], rope_ref[pl.ds(r0, BQ), 0:LANE], rope_ref[pl.ds(r0, BQ), LANE:2 * LANE])
            ss_rope = jnp.sum(k_rope * k_rope, axis=-1, keepdims=True)
            for h in range(A_HEADS):
                hs = slice(h * HEAD_PAD, (h + 1) * HEAD_PAD)
                kh = kv[:, hs]
                r = lax.rsqrt((jnp.sum(kh * kh, axis=-1, keepdims=True) + ss_rope) * inv_qk + EPS)
                k_scr[j, hs, :] = ((kh * khg_ref[...] + kr) * r).T.astype(_BF)
            return carry

        lax.fori_loop(0, S // BQ, kv_block, 0)

    q0 = pl.multiple_of(qi * BQ, BQ)
    cq = cq_ref[...].astype(_F32)
    qn = (cq * lax.rsqrt(jnp.mean(cq * cq, axis=-1, keepdims=True) + EPS) * qg_ref[...]).astype(_BF)
    qall = _dot(qn, wuq_ref[...])
    g_direct = qhg_ref[0:1, :] * rope_ref[pl.ds(q0, BQ), 0:LANE]
    g_swapped = qhg_ref[1:2, :] * rope_ref[pl.ds(q0, BQ), LANE:2 * LANE]
    scale = (A_QK_DIM ** -0.5) * 1.4426950408889634
    for h in range(A_HEADS):
        hs = slice(h * HEAD_PAD, (h + 1) * HEAD_PAD)
        qh = qall[:, hs]
        qs = qall[:, A_PAD + h * HEAD_PAD:A_PAD + (h + 1) * HEAD_PAD]
        r = lax.rsqrt(jnp.sum(qh * qh, axis=-1, keepdims=True) * inv_qk + EPS) * scale
        q_scr[h][...] = ((qh * g_direct + qs * g_swapped) * r).astype(_BF)
        m_scr[h][...] = jnp.full((BQ, LANE), NEG, _F32)
        acc_scr[h][...] = jnp.zeros((BQ, LANE), _F32)
    col_minus_row = lax.broadcasted_iota(jnp.int32, (BQ, BQ), 1) - lax.broadcasted_iota(jnp.int32, (BQ, BQ), 0)

    def kblock(j, masked):
        k0 = pl.multiple_of(j * BQ, BQ)
        scores = [_dot(q_scr[h][...], k_scr[j, h * HEAD_PAD:(h + 1) * HEAD_PAD, :]) for h in range(A_HEADS)]
        probs = []
        for h in range(A_HEADS):
            s = scores[h]
            if masked:
                s = jnp.where(col_minus_row <= 0, s, NEG)
            m_prev = m_scr[h][...]
            m_new = jnp.maximum(m_prev, jnp.max(s, axis=-1, keepdims=True))
            m_scr[h][...] = m_new
            probs.append((jnp.exp2(m_prev - m_new),
                          jnp.exp2(s - jnp.concatenate([m_new] * (BQ // LANE), axis=1)).astype(_BF)))
        for h in range(A_HEADS):
            hs = slice(h * HEAD_PAD, (h + 1) * HEAD_PAD)
            alpha, pe = probs[h]
            acc_scr[h][...] = alpha * acc_scr[h][...] + _dot(pe, v_scr[pl.ds(k0, BQ), hs])

    def full_block(j, carry):
        kblock(j, False)
        return carry

    def diagonal_block(j, carry):
        kblock(j, True)
        return carry

    lax.fori_loop(0, qi, full_block, 0)
    lax.fori_loop(qi, qi + 1, diagonal_block, 0)

    lo = lane < A_V_DIM
    for p in range(A_HEADS // 2):
        ps = slice(p * LANE, (p + 1) * LANE)
        even = acc_scr[2 * p][...]
        odd = acc_scr[2 * p + 1][...]
        o = jnp.where(lo, even / even[:, A_V_DIM:A_V_DIM + 1], pltpu.roll(odd / odd[:, A_V_DIM:A_V_DIM + 1], A_V_DIM, 1))
        o2 = o * o
        ms_lo = jnp.sum(jnp.where(lo, o2, 0.0), axis=-1, keepdims=True) * (1.0 / A_V_DIM)
        ms_hi = jnp.sum(jnp.where(lo, 0.0, o2), axis=-1, keepdims=True) * (1.0 / A_V_DIM)
        rs = jnp.where(lo, lax.rsqrt(ms_lo + EPS), lax.rsqrt(ms_hi + EPS))
        y_ref[:, ps] = (o * rs * ag_ref[:, ps]).astype(_BF)


def _mla(cq, ckr, rope, lw, B, S):
    T = B * S
    BQ = ATT_BLOCK
    nq = S // BQ
    return pl.pallas_call(
        _mla_kernel,
        grid=(B, nq),
        in_specs=[pl.BlockSpec((BQ, A_Q_RANK), lambda b, q: (b * nq + q, 0)),
                  pl.BlockSpec((S, 2 * LANE), lambda b, q: (b, 0)),
                  pl.BlockSpec((S, 2 * LANE), lambda b, q: (b, 0)),
                  _const_spec((1, A_Q_RANK)), _const_spec((1, A_KV_RANK)),
                  _const_spec((A_Q_RANK, 2 * A_PAD)), _const_spec((A_KV_RANK, 2 * A_PAD)),
                  _const_spec((2, LANE)), _const_spec((1, LANE)), _const_spec((1, A_WIDTH))],
        out_specs=pl.BlockSpec((BQ, A_WIDTH), lambda b, q: (b * nq + q, 0)),
        out_shape=jax.ShapeDtypeStruct((T, A_WIDTH), _BF),
        scratch_shapes=([pltpu.VMEM((nq, A_PAD, BQ), _BF), pltpu.VMEM((S, A_PAD), _BF)]
                        + [pltpu.VMEM((BQ, HEAD_PAD), _BF)] * A_HEADS + [pltpu.VMEM((BQ, LANE), _F32)] * (2 * A_HEADS)),
        compiler_params=_params(2),
        name="mla",
    )(cq, ckr, rope, lw["a_q_norm"], lw["a_kv_norm"], lw["a_w_uq"], lw["a_w_ukv"],
      lw["a_q_head_norm"], lw["a_k_head_norm"], lw["a_out_norm"])


def _outproj_router_kernel(ym_ref, ya_ref, yc_ref, x_ref, wm_ref, wa_ref, wc_ref, g_ref, wr_hi_ref, wr_lo_ref,
                           br_ref, tri_ref, xmid_ref, ri_ref, rw_ref, cnt_ref, *, tiles_per_group):
    tm = x_ref.shape[0]
    i = pl.program_id(0)

    @pl.when(i % tiles_per_group == 0)
    def _():
        cnt_ref[...] = jnp.zeros_like(cnt_ref)

    y = _dot(ym_ref[...], wm_ref[...]) + _dot(ya_ref[...], wa_ref[...]) + _dot(yc_ref[...], wc_ref[...])
    xm = x_ref[...] + y
    xmid_ref[...] = xm
    xn = xm * lax.rsqrt(jnp.mean(xm * xm, axis=-1, keepdims=True) + EPS) * g_ref[...]

    xh, xl = _split2(xn)
    logits = _dot(xh, wr_hi_ref[...]) + _dot(xh, wr_lo_ref[...]) + _dot(xl, wr_hi_ref[...])
    biased = logits + br_ref[...]
    lane = lax.broadcasted_iota(jnp.int32, (tm, LANE), 1)
    lane_f = lane.astype(_F32)

    def lane_max(a):
        return jnp.max(a, axis=-1, keepdims=True)

    def lane_sum(a):
        return jnp.sum(a, axis=-1, keepdims=True)

    def first_argmax(a, amax):
        return jnp.min(jnp.where(a == amax, lane_f, float(LANE)), axis=-1, keepdims=True).astype(jnp.int32)

    gmask = lane < N_GROUPS
    gl = jnp.where(gmask, logits, NEG)
    gexp = jnp.where(gmask, jnp.exp(gl - lane_max(gl)), 0.0)
    gsc = jnp.where(gmask, biased, NEG)
    g_sel = first_argmax(gsc, lane_max(gsc))
    g_gate = lane_sum(jnp.where(lane == g_sel, gexp, 0.0)) / lane_sum(gexp)

    e_lo = N_GROUPS + EXPERTS_PER_GROUP * g_sel
    emask = jnp.logical_and(lane >= e_lo, lane < e_lo + EXPERTS_PER_GROUP)
    el = jnp.where(emask, logits, NEG)
    eexp = jnp.where(emask, jnp.exp(el - lane_max(el)), 0.0)
    esc = jnp.where(emask, biased, NEG)
    i1 = first_argmax(esc, lane_max(esc))
    esc2 = jnp.where(lane == i1, NEG, esc)
    i2 = first_argmax(esc2, lane_max(esc2))
    oh1 = lane == i1
    oh2 = lane == i2
    p1 = lane_sum(jnp.where(oh1, eexp, 0.0))
    p2 = lane_sum(jnp.where(oh2, eexp, 0.0))
    w1 = g_gate * p1 / (p1 + p2)
    w2 = g_gate * p2 / (p1 + p2)

    cnt = jnp.where(oh1, 1.0, 0.0) + jnp.where(oh2, 1.0, 0.0)
    base = cnt_ref[0] + _dot(tri_ref[...], cnt.astype(_BF))
    rank1 = lane_sum(jnp.where(oh1, base, 0.0)).astype(jnp.int32)
    rank2 = lane_sum(jnp.where(oh2, base, 0.0)).astype(jnp.int32)
    cnt_ref[0] = cnt_ref[0] + jnp.sum(cnt, axis=0, keepdims=True)

    code1 = (i1 - N_GROUPS) * RANK_SPAN + rank1
    code2 = (i2 - N_GROUPS) * RANK_SPAN + rank2
    ri_ref[...] = jnp.where(lane == 0, code1, jnp.where(lane == 1, code2, 0))
    rw_ref[...] = jnp.where(lane == 0, w1, jnp.where(lane == 1, w2, 0.0))


def _outproj_router(ym, ya, yc, x, lw, tm, tg):
    T = x.shape[0]
    tiles_per_group = tg // tm

    def tok(width):
        return pl.BlockSpec((tm, width), lambda i: (i, 0))

    return pl.pallas_call(
        functools.partial(_outproj_router_kernel, tiles_per_group=tiles_per_group),
        grid=(T // tm,),
        in_specs=[tok(M_PAD), tok(A_WIDTH), tok(C_WIDTH), tok(D_MODEL),
                  _const_spec((M_PAD, D_MODEL)), _const_spec((A_WIDTH, D_MODEL)), _const_spec((C_WIDTH, D_MODEL)),
                  _const_spec((1, D_MODEL)), _const_spec((D_MODEL, LANE)), _const_spec((D_MODEL, LANE)),
                  _const_spec((1, LANE)), _const_spec((tm, tm))],
        out_specs=[tok(D_MODEL), tok(LANE), tok(LANE),
                   pl.BlockSpec((1, 1, LANE), lambda i: (i // tiles_per_group, 0, 0))],
        out_shape=[jax.ShapeDtypeStruct((T, D_MODEL), _F32),
                   jax.ShapeDtypeStruct((T, LANE), jnp.int32), jax.ShapeDtypeStruct((T, LANE), _F32),
                   jax.ShapeDtypeStruct((T // tg, 1, LANE), _F32)],
        compiler_params=_params(1),
        name="outproj_router",
    )(ym, ya, yc, x, lw["w_out_m"], lw["w_out_a"], lw["w_out_c"], lw["ffn_norm"], lw["w_router_hi"],
      lw["w_router_lo"], lw["b_router"], lw["tri_strict"])


SLAB = 8
MOE_CHUNK = 160
MOE_ROWS = 8
CONV_ROWS = 256
INVERT_UNROLL = 8
MOE_PREP_ROWS = 512
EXPERTS_PER_STEP = 4
MOE_VMEM_LIMIT = 56 * 1024 * 1024


def _moe_kernel(cnt_ref, code_ref, w_ref, xmid_ref, g_ref, w1_ref, w3_ref, w2_ref, out_ref,
                xs_ref, acc_ref, g_buf, y_buf, inv_ref):
    i = pl.program_id(0)
    sid = pl.program_id(1)
    tg = out_ref.shape[0]
    CH = MOE_CHUNK

    def slab_rows(r0, j, n):
        return pl.ds(pl.multiple_of(r0 * SLAB, SLAB) + j, n, stride=SLAB)

    def slab(row):
        return pl.ds(pl.multiple_of(row * SLAB, SLAB), SLAB)

    prep_rows = xmid_ref.shape[0]
    n_prep = tg // prep_rows
    step = sid - n_prep

    @pl.when(jnp.logical_and(i == 0, sid == 0))
    def _():
        g_buf[...] = jnp.zeros_like(g_buf)

    @pl.when(sid == 0)
    def _():
        acc_ref[slab(tg), :] = jnp.zeros((SLAB, LANE), _F32)

        def clear_tail(ex, carry):
            t0 = ex * RANK_SPAN + cnt_ref[i * N_EXPERTS + ex]
            for r in range(MOE_ROWS):
                inv_ref[t0 + r] = 0
            return carry

        lax.fori_loop(0, N_EXPERTS, clear_tail, 0)

    @pl.when(sid < n_prep)
    def _prepare_rows():
        row0 = sid * prep_rows

        def convert(rb, carry):
            r0 = pl.multiple_of(rb * CONV_ROWS, CONV_ROWS)
            xm = xmid_ref[pl.ds(r0, CONV_ROWS), :]
            xn = xm * lax.rsqrt(jnp.mean(xm * xm, axis=-1, keepdims=True) + EPS) * g_ref[...]
            g0 = pl.multiple_of(row0 + r0, CONV_ROWS)
            for j in range(SLAB):
                xs_ref[slab_rows(g0, j, CONV_ROWS), :] = xn[:, j * LANE:(j + 1) * LANE]
                acc_ref[slab_rows(g0, j, CONV_ROWS), :] = xm[:, j * LANE:(j + 1) * LANE]
            a0 = 2 * (row0 + r0)
            for it in range(2 * CONV_ROWS // INVERT_UNROLL):
                codes = [code_ref[0, 0, a0 + it * INVERT_UNROLL + k] for k in range(INVERT_UNROLL)]
                for k, code in enumerate(codes):
                    inv_ref[code] = a0 + it * INVERT_UNROLL + k
            return carry

        lax.fori_loop(0, prep_rows // CONV_ROWS, convert, 0)

    def chunk(c, ee):
        e = step * EXPERTS_PER_STEP + ee
        base = e * RANK_SPAN
        end = base + cnt_ref[i * N_EXPERTS + e]
        b0 = base + c * CH
        n_rows = jnp.minimum(end - b0, CH)
        n_full = n_rows // MOE_ROWS

        def gather(it, cc):
            p0 = b0 + it * MOE_ROWS
            d0 = pl.multiple_of(it * (MOE_ROWS * SLAB), MOE_ROWS * SLAB)
            for r in range(MOE_ROWS):
                tok = lax.shift_right_logical(inv_ref[p0 + r], 1)
                g_buf[pl.ds(d0 + r * SLAB, SLAB), :] = xs_ref[slab(tok), :]
            return cc

        lax.fori_loop(0, (n_rows + MOE_ROWS - 1) // MOE_ROWS, gather, 0)
        x = jnp.concatenate([g_buf[slab_rows(0, j, CH), :] for j in range(SLAB)], axis=1).astype(_BF)
        hid = (jax.nn.silu(_dot(x, w1_ref[ee])) * _dot(x, w3_ref[ee])).astype(_BF)
        y = _dot(hid, w2_ref[ee])
        for j in range(SLAB):
            y_buf[slab_rows(0, j, CH), :] = y[:, j * LANE:(j + 1) * LANE]

        def scatter(it, masked):
            p0 = b0 + it * MOE_ROWS
            d0 = pl.multiple_of(it * (MOE_ROWS * SLAB), MOE_ROWS * SLAB)
            updates = []
            for r in range(MOE_ROWS):
                a = inv_ref[p0 + r]
                tok = lax.shift_right_logical(a, 1)
                wgt = w_ref[0, 0, a]
                if masked:
                    ok = p0 + r < end
                    tok = jnp.where(ok, tok, tg)
                    wgt = jnp.where(ok, wgt, 0.0)
                updates.append((tok, acc_ref[slab(tok), :] + wgt * y_buf[pl.ds(d0 + r * SLAB, SLAB), :]))
            for tok, val in updates:
                acc_ref[slab(tok), :] = val

        def scatter_full(it, cc):
            scatter(it, False)
            return cc

        lax.fori_loop(0, n_full, scatter_full, 0)

        @pl.when(n_rows > n_full * MOE_ROWS)
        def _():
            scatter(n_full, True)

    @pl.when(step >= 0)
    def _run_experts():
        for ee in range(EXPERTS_PER_STEP):
            def expert_chunks(c, carry, ee=ee):
                chunk(c, ee)
                return carry

            count = cnt_ref[i * N_EXPERTS + step * EXPERTS_PER_STEP + ee]
            lax.fori_loop(0, (count + CH - 1) // CH, expert_chunks, 0)

    @pl.when(step == N_EXPERTS // EXPERTS_PER_STEP - 1)
    def _finish_group():
        def convert(rb, carry):
            r0 = pl.multiple_of(rb * CONV_ROWS, CONV_ROWS)
            for j in range(SLAB):
                out_ref[pl.ds(r0, CONV_ROWS), j * LANE:(j + 1) * LANE] = acc_ref[slab_rows(r0, j, CONV_ROWS), :]
            return carry

        lax.fori_loop(0, tg // CONV_ROWS, convert, 0)


def _moe(xmid, ri, rw, counts, lw, tg):
    T = xmid.shape[0]
    ng = T // tg
    assert tg <= RANK_SPAN
    cnt = counts[:, 0, N_GROUPS:N_GROUPS + N_EXPERTS].astype(jnp.int32).reshape(ng * N_EXPERTS)
    per_group = lambda a: a.reshape(ng, 1, 2 * tg)
    smem = pl.BlockSpec((1, 1, 2 * tg), lambda i, s, cn: (i, 0, 0), memory_space=pltpu.SMEM)
    prep_rows = min(MOE_PREP_ROWS, tg)
    n_prep = tg // prep_rows
    n_steps = n_prep + N_EXPERTS // EXPERTS_PER_STEP
    expert_block = lambda i, s, cn: (jnp.maximum(s - n_prep, 0), 0, 0)
    grid_spec = pltpu.PrefetchScalarGridSpec(
        num_scalar_prefetch=1,
        grid=(ng, n_steps),
        in_specs=[smem, smem,
                  pl.BlockSpec((prep_rows, D_MODEL), lambda i, s, cn: (i * n_prep + jnp.minimum(s, n_prep - 1), 0)),
                  pl.BlockSpec((1, D_MODEL), lambda i, s, cn: (0, 0)),
                  pl.BlockSpec((EXPERTS_PER_STEP, D_MODEL, D_EXPERT), expert_block),
                  pl.BlockSpec((EXPERTS_PER_STEP, D_MODEL, D_EXPERT), expert_block),
                  pl.BlockSpec((EXPERTS_PER_STEP, D_EXPERT, D_MODEL), expert_block)],
        out_specs=pl.BlockSpec((tg, D_MODEL), lambda i, s, cn: (i, 0)),
        scratch_shapes=[pltpu.VMEM((tg * SLAB, LANE), _F32), pltpu.VMEM(((tg + 1) * SLAB, LANE), _F32),
                        pltpu.VMEM((MOE_CHUNK * SLAB, LANE), _F32), pltpu.VMEM((MOE_CHUNK * SLAB, LANE), _F32),
                        pltpu.SMEM((N_EXPERTS * RANK_SPAN + MOE_ROWS,), jnp.int32)],
    )
    return pl.pallas_call(
        _moe_kernel,
        grid_spec=grid_spec,
        out_shape=jax.ShapeDtypeStruct((T, D_MODEL), _F32),
        compiler_params=pltpu.CompilerParams(dimension_semantics=("arbitrary", "arbitrary"),
                                             vmem_limit_bytes=MOE_VMEM_LIMIT),
        name="moe",
    )(cnt, per_group(ri[:, 0:2]), per_group(rw[:, 0:2]), xmid, lw["ffn_norm"], lw["w1"], lw["w3"], lw["w2"])


def _pad_heads(w, n_heads, dim):
    lead = w.shape[:-1]
    w = w.reshape(lead + (n_heads, dim))
    w = jnp.pad(w, [(0, 0)] * len(lead) + [(0, 0), (0, HEAD_PAD - dim)])
    return w.reshape(lead + (n_heads * HEAD_PAD,))


def _swap_rope_halves(w):
    lead = w.shape[:-1]
    w = w.reshape(lead + (-1, HEAD_PAD))
    half = A_ROPE // 2
    w = jnp.concatenate([w[..., :A_NOPE], w[..., A_NOPE + half:A_NOPE + A_ROPE], w[..., A_NOPE:A_NOPE + half],
                         w[..., A_NOPE + A_ROPE:]], axis=-1)
    return w.reshape(lead + (-1,))


def _pack_layer(p, l, tm):
    w_in = p["w_in"][l]
    sizes = (M_WIDTH, M_WIDTH, M_WIDTH, M_WIDTH, M_HEADS, M_HEADS, A_Q_RANK, A_KV_RANK, A_ROPE, C_WIDTH, C_WIDTH)
    cols, off = [], 0
    for s in sizes:
        cols.append(w_in[:, off:off + s])
        off += s
    mq, mk, mv, mo, mi, mf, cq, ckv, krope, cu, cv = cols
    zeros = lambda n: jnp.zeros((D_MODEL, n), _F32)
    w_all = jnp.concatenate([
        _pad_heads(mq, M_HEADS, M_HEAD_DIM), _pad_heads(mk, M_HEADS, M_HEAD_DIM),
        _pad_heads(mv, M_HEADS, M_HEAD_DIM), _pad_heads(mo, M_HEADS, M_HEAD_DIM),
        mi, mf, zeros(LANE - 2 * M_HEADS),
        cq, ckv, zeros(A_NOPE), krope, zeros(LANE - A_NOPE - A_ROPE), cu, cv], axis=1).astype(_BF)

    g = p["mix_out_norm"][l]
    w_out = p["w_out"][l]
    w_out_m = jnp.pad(w_out[:M_WIDTH].reshape(M_HEADS, M_HEAD_DIM, D_MODEL),
                      ((0, 0), (0, HEAD_PAD - M_HEAD_DIM), (0, 0))).reshape(M_PAD, D_MODEL)

    w_uq = _pad_heads(p["a_w_uq"][l], A_HEADS, A_QK_DIM)
    w_uq = jnp.concatenate([w_uq, _swap_rope_halves(w_uq)], axis=1)
    w_ukv = p["a_w_ukv"][l].reshape(A_KV_RANK, A_HEADS, A_NOPE + A_V_DIM)
    w_uk = _pad_heads(w_ukv[:, :, :A_NOPE].reshape(A_KV_RANK, A_HEADS * A_NOPE), A_HEADS, A_NOPE)
    w_uv = _pad_heads(w_ukv[:, :, A_NOPE:].reshape(A_KV_RANK, A_WIDTH), A_HEADS, A_V_DIM)
    q_head_gain = jnp.pad(p["a_q_head_norm"][l], (0, HEAD_PAD - A_QK_DIM)).reshape(1, HEAD_PAD)

    w_router = jnp.concatenate([p["w_group"][l], p["w_expert"][l],
                                jnp.zeros((D_MODEL, LANE - N_GROUPS - N_EXPERTS), _F32)], axis=1)
    wr_hi = w_router.astype(_BF)
    wr_lo = (w_router - wr_hi.astype(_F32)).astype(_BF)
    b_router = jnp.concatenate([p["b_group"][l], p["b_expert"][l],
                                jnp.zeros((LANE - N_GROUPS - N_EXPERTS,), _F32)]).reshape(1, LANE)

    tril = jnp.tril(jnp.ones((C_CHUNK, C_CHUNK), _F32))
    lane_pad = lambda a: jnp.pad(a, (0, HEAD_PAD - a.shape[0])).reshape(1, HEAD_PAD)
    return {
        "attn_norm": p["attn_norm"][l].reshape(1, D_MODEL),
        "w_in": w_all,
        "c_v_norm": p["c_v_norm"][l].reshape(1, C_WIDTH),
        "c_w_s": (p["c_w_s"][l] * tril).astype(_BF),
        "c_b_s": jnp.repeat(p["c_b_s"][l].T, C_GROUP_DIM, axis=1),
        "c_out_norm": g[M_WIDTH + A_WIDTH:].reshape(1, C_WIDTH),
        "ones_bd": jnp.kron(jnp.eye(C_GROUPS, dtype=_F32), jnp.ones((C_GROUP_DIM, C_GROUP_DIM), _F32)).astype(_BF),
        "m_conv_w": jnp.concatenate([_pad_heads(p["m_conv_w"][l][:, :M_WIDTH], M_HEADS, M_HEAD_DIM),
                                     _pad_heads(p["m_conv_w"][l][:, M_WIDTH:], M_HEADS, M_HEAD_DIM)], axis=1),
        "m_conv_b": jnp.concatenate([_pad_heads(p["m_conv_b"][l][:M_WIDTH], M_HEADS, M_HEAD_DIM),
                                     _pad_heads(p["m_conv_b"][l][M_WIDTH:], M_HEADS, M_HEAD_DIM)]).reshape(1, 2 * M_PAD),
        "m_gate_bias": jnp.pad(p["m_gate_bias"][l], (0, LANE - 2 * M_HEADS)).reshape(1, LANE),
        "m_out_norm": _pad_heads(g[:M_WIDTH], M_HEADS, M_HEAD_DIM).reshape(1, M_PAD),
        "tri_incl": tril.astype(_BF),
        "a_q_norm": p["a_q_norm"][l].reshape(1, A_Q_RANK),
        "a_kv_norm": p["a_kv_norm"][l].reshape(1, A_KV_RANK),
        "a_w_uq": w_uq.astype(_BF),
        "a_w_ukv": jnp.concatenate([w_uk, w_uv], axis=1).astype(_BF),
        "a_q_head_norm": jnp.concatenate([q_head_gain, _swap_rope_halves(q_head_gain)], axis=0),
        "a_k_head_norm": lane_pad(p["a_k_head_norm"][l]),
        "a_out_norm": g[M_WIDTH:M_WIDTH + A_WIDTH].reshape(1, A_WIDTH),
        "w_out_m": w_out_m.astype(_BF),
        "w_out_a": w_out[M_WIDTH:M_WIDTH + A_WIDTH].astype(_BF),
        "w_out_c": w_out[M_WIDTH + A_WIDTH:].astype(_BF),
        "ffn_norm": p["ffn_norm"][l].reshape(1, D_MODEL),
        "w_router_hi": wr_hi,
        "w_router_lo": wr_lo,
        "b_router": b_router,
        "tri_strict": jnp.tril(jnp.ones((tm, tm), _F32), -1).astype(_BF),
        "w1": p["w1"][l].astype(_BF),
        "w3": p["w3"][l].astype(_BF),
        "w2": p["w2"][l].astype(_BF),
    }


def _layer(x, rope, lw, B, S, tm, tg):
    qk, v, o, gates, cq, ckr, yc = _inproj(x, lw, tm)
    ym = _mlstm(qk, v, o, gates, lw, B, S)
    ya = _mla(cq, ckr, rope, lw, B, S)
    xmid, ri, rw, counts = _outproj_router(ym, ya, yc, x, lw, tm, tg)
    return _moe(xmid, ri, rw, counts, lw, tg)


def kernel(x, positions, attn_norm, w_in, m_conv_w, m_conv_b, m_gate_bias, a_q_norm, a_kv_norm, a_w_uq, a_w_ukv, a_q_head_norm, a_k_head_norm, c_v_norm, c_w_s, c_b_s, mix_out_norm, w_out, ffn_norm, w_group, b_group, w_expert, b_expert, w1, w3, w2):
    B, S, D = x.shape
    assert D == D_MODEL and S % ATT_BLOCK == 0
    T = B * S
    tm = _token_tile(T)
    tg = _moe_group(T, tm)
    p = dict(attn_norm=attn_norm, w_in=w_in, m_conv_w=m_conv_w, m_conv_b=m_conv_b, m_gate_bias=m_gate_bias,
             a_q_norm=a_q_norm, a_kv_norm=a_kv_norm, a_w_uq=a_w_uq, a_w_ukv=a_w_ukv, a_q_head_norm=a_q_head_norm,
             a_k_head_norm=a_k_head_norm, c_v_norm=c_v_norm, c_w_s=c_w_s, c_b_s=c_b_s, mix_out_norm=mix_out_norm,
             w_out=w_out, ffn_norm=ffn_norm, w_group=w_group, b_group=b_group, w_expert=w_expert,
             b_expert=b_expert, w1=w1, w3=w3, w2=w2)
    rope = _rope_tables(positions)
    xt = x.reshape(T, D)
    for l in range(w_in.shape[0]):
        xt = _layer(xt, rope, _pack_layer(p, l, tm), B, S, tm, tg)
    return xt.reshape(B, S, D)
```
